```python
import jax, jax.numpy as jnp
from jax import lax
import numpy as np

D_MODEL = 4096
BATCH = 4
SEQ = 2048
DEPTH = 2
DEC_BATCH = 8
DEC_SEQ = 1
PAST_LEN = 16384
PAGE_SIZE = 128

N_MIXERS = 2
N_ATTN_LAYERS = (DEPTH + 1) // 2
N_RET_LAYERS = DEPTH // 2
ATTN_HEADS = 32
HEAD_DIM = 128
ATTN_WIDTH = ATTN_HEADS * HEAD_DIM
MOBA_BLOCK = 256
MOBA_TOPK = 3
MOBA_QCHUNK = 8
ROPE_THETA = 10000.0
RET_HEADS = 16
RET_QK_DIM = D_MODEL // RET_HEADS
RET_V_DIM = 2 * D_MODEL // RET_HEADS
RET_QK_WIDTH = RET_HEADS * RET_QK_DIM
RET_V_WIDTH = RET_HEADS * RET_V_DIM
RET_CHUNK = 128
EPS = 1e-6

kernel_name = 'moba_retention_hybrid_step'


def rmsnorm(x, g):
    xf = x.astype(jnp.float32)
    y = xf * lax.rsqrt(jnp.mean(xf * xf, axis=-1, keepdims=True) + EPS) * g.astype(jnp.float32)
    return y.astype(x.dtype)


def rope(x, pos):
    d = x.shape[-1]
    half = d // 2
    inv = ROPE_THETA ** (-jnp.arange(half, dtype=jnp.float32) / half)
    ang = pos.astype(jnp.float32)[:, None] * inv[None, :]
    cos = jnp.cos(ang)[None, :, None, :]
    sin = jnp.sin(ang)[None, :, None, :]
    xf = x.astype(jnp.float32)
    x1, x2 = xf[..., :half], xf[..., half:]
    return jnp.concatenate([x1 * cos - x2 * sin, x2 * cos + x1 * sin], axis=-1).astype(x.dtype)


def moba_attention(q, k, v, q_start):
    B, L, H, D = q.shape
    T = k.shape[1]
    nb = -(-T // MOBA_BLOCK)
    pad = nb * MOBA_BLOCK - T
    k = jnp.pad(k, ((0, 0), (0, pad), (0, 0), (0, 0)))
    v = jnp.pad(v, ((0, 0), (0, pad), (0, 0), (0, 0)))
    kb = k.reshape(B, nb, MOBA_BLOCK, H, D)
    vb = v.reshape(B, nb, MOBA_BLOCK, H, D)
    kmean = jnp.mean(kb.astype(jnp.float32), axis=2)
    n_sel = min(MOBA_TOPK, nb)
    qc = MOBA_QCHUNK if L % MOBA_QCHUNK == 0 else L
    nq = L // qc
    qs = q.reshape(B, nq, qc, H, D).swapaxes(0, 1)
    pos = (q_start + jnp.arange(L, dtype=jnp.int32)).reshape(nq, qc)
    bi = jnp.arange(B)[:, None, None, None]
    hi = jnp.arange(H)[None, None, :, None]
    blk_ids = jnp.arange(nb, dtype=jnp.int32)
    key_off = jnp.arange(MOBA_BLOCK, dtype=jnp.int32)
    scale = D ** -0.5

    def one_chunk(args):
        qcb, p = args
        qf = qcb.astype(jnp.float32)
        own = p // MOBA_BLOCK
        gate = jnp.einsum('bqhd,bnhd->bqhn', qf, kmean)
        past = blk_ids[None, :] < own[:, None]
        gate = jnp.where(past[None, :, None, :], gate, -jnp.inf)
        _, sel = lax.top_k(gate, n_sel)
        own_b = jnp.broadcast_to(own[None, :, None, None], (B, qc, H, 1))
        blocks = jnp.concatenate([sel, own_b], axis=-1)
        ok = jnp.concatenate([sel < own[None, :, None, None], jnp.ones((B, qc, H, 1), bool)], axis=-1)
        kg = kb[bi, blocks, :, hi]
        vg = vb[bi, blocks, :, hi]
        s = jnp.einsum('bqhd,bqhskd->bqhsk', qf, kg.astype(jnp.float32)) * scale
        kpos = blocks[..., None] * MOBA_BLOCK + key_off
        mask = ok[..., None] & (kpos <= p[None, :, None, None, None])
        s = jnp.where(mask, s, -jnp.inf)
        S = s.shape[3]
        prob = jax.nn.softmax(s.reshape(B, qc, H, S * MOBA_BLOCK), axis=-1).reshape(B, qc, H, S, MOBA_BLOCK)
        o = jnp.einsum('bqhsk,bqhskd->bqhd', prob, vg.astype(jnp.float32))
        return o.astype(q.dtype)

    out = lax.map(one_chunk, (qs, pos))
    return out.swapaxes(0, 1).reshape(B, L, H, D)


def moba_layer(x, pos0, k_past, v_past, norm_g, w_in, qn, kn, w_out):
    B, L, _ = x.shape
    h = rmsnorm(x, norm_g)
    z = jnp.einsum('bld,de->ble', h, w_in)
    q, k, v, g = jnp.split(z, 4, axis=-1)
    pos = pos0 + jnp.arange(L, dtype=jnp.int32)
    q = rope(rmsnorm(q.reshape(B, L, ATTN_HEADS, HEAD_DIM), qn), pos)
    k = rope(rmsnorm(k.reshape(B, L, ATTN_HEADS, HEAD_DIM), kn), pos)
    v = v.reshape(B, L, ATTN_HEADS, HEAD_DIM)
    if k_past is None:
        k_all, v_all = k, v
    else:
        k_all = jnp.concatenate([k_past.astype(k.dtype), k], axis=1)
        v_all = jnp.concatenate([v_past.astype(v.dtype), v], axis=1)
    o = moba_attention(q, k_all, v_all, pos0).reshape(B, L, ATTN_WIDTH)
    o = (o * jax.nn.silu(g)).astype(x.dtype)
    return x + jnp.einsum('ble,ed->bld', o, w_out), k, v


def retention(q, k, v, s0):
    B, L, H, DK = q.shape
    DV = v.shape[-1]
    c = RET_CHUNK if L % RET_CHUNK == 0 else L
    n = L // c
    log_g = jnp.log1p(-jnp.exp2(-5.0 - jnp.arange(H, dtype=jnp.float32)))
    i = jnp.arange(c, dtype=jnp.float32)
    diff = i[:, None] - i[None, :]
    dmask = jnp.where(diff >= 0, jnp.exp(log_g[:, None, None] * jnp.maximum(diff, 0.0)), 0.0)
    q_dec = jnp.exp(log_g[None, :] * (i[:, None] + 1.0))
    k_dec = jnp.exp(log_g[None, :] * (c - 1.0 - i[:, None]))
    chunk_dec = jnp.exp(log_g * c)[:, None, None]

    def to_chunks(t):
        return t.astype(jnp.float32).reshape(B, n, c, H, t.shape[-1]).swapaxes(0, 1)

    def step(s, inp):
        qc, kc, vc = inp
        att = jnp.einsum('bihd,bjhd->bhij', qc, kc) * dmask
        o = (jnp.einsum('bhij,bjhv->bihv', att, vc)
             + jnp.einsum('bihd,bhdv->bihv', qc * q_dec[:, :, None], s))
        s = s * chunk_dec + jnp.einsum('bjhd,bjhv->bhdv', kc * k_dec[:, :, None], vc)
        return s, o

    s, o = lax.scan(step, s0.astype(jnp.float32), (to_chunks(q), to_chunks(k), to_chunks(v)))
    return o.swapaxes(0, 1).reshape(B, L, H, DV), s


def retention_layer(x, pos0, s0, norm_g, w_in, head_g, w_out):
    B, L, _ = x.shape
    h = rmsnorm(x, norm_g)
    z = jnp.einsum('bld,de->ble', h, w_in)
    q, k, v, g = jnp.split(z, [RET_QK_WIDTH, 2 * RET_QK_WIDTH, 2 * RET_QK_WIDTH + RET_V_WIDTH], axis=-1)
    pos = pos0 + jnp.arange(L, dtype=jnp.int32)
    q = rope(q.reshape(B, L, RET_HEADS, RET_QK_DIM), pos)
    k = rope(k.reshape(B, L, RET_HEADS, RET_QK_DIM), pos) * (RET_QK_DIM ** -0.5)
    v = v.reshape(B, L, RET_HEADS, RET_V_DIM)
    o, s = retention(q, k, v, s0)
    o = rmsnorm(o, head_g).reshape(B, L, RET_V_WIDTH)
    o = (o * jax.nn.silu(g).astype(jnp.float32)).astype(x.dtype)
    return x + jnp.einsum('ble,ed->bld', o, w_out), s


def setup_inputs(seed: int = 0) -> dict:
    key = jax.random.key(seed)
    ks = jax.random.split(key, 16)
    f32 = jnp.float32
    n_pages = PAST_LEN // PAGE_SIZE
    n_used = DEC_BATCH * n_pages
    n_phys = n_used + n_used // 4
    x_prompt = jax.random.normal(ks[0], (BATCH, SEQ, D_MODEL), f32)
    x_sample = jax.random.normal(ks[1], (DEC_BATCH, DEC_SEQ, D_MODEL), f32)
    cache_k = jax.random.normal(ks[2], (N_ATTN_LAYERS, n_phys, PAGE_SIZE, ATTN_HEADS, HEAD_DIM), f32)
    cache_v = jax.random.normal(ks[3], (N_ATTN_LAYERS, n_phys, PAGE_SIZE, ATTN_HEADS, HEAD_DIM), f32)
    state_ret = 0.5 * jax.random.normal(ks[4], (N_RET_LAYERS, DEC_BATCH, RET_HEADS, RET_QK_DIM, RET_V_DIM), f32)
    page_table = jax.random.permutation(ks[5], n_phys)[:n_used].reshape(DEC_BATCH, n_pages).astype(jnp.int32)
    attn_norm = 1.0 + 0.02 * jax.random.normal(ks[6], (N_ATTN_LAYERS, D_MODEL), f32)
    w_in_attn = jax.random.normal(ks[7], (N_ATTN_LAYERS, D_MODEL, 4 * ATTN_WIDTH), f32) * D_MODEL ** -0.5
    q_norm = 1.0 + 0.02 * jax.random.normal(ks[8], (N_ATTN_LAYERS, HEAD_DIM), f32)
    k_norm = 1.0 + 0.02 * jax.random.normal(ks[9], (N_ATTN_LAYERS, HEAD_DIM), f32)
    w_out_attn = jax.random.normal(ks[10], (N_ATTN_LAYERS, ATTN_WIDTH, D_MODEL), f32) * ATTN_WIDTH ** -0.5
    ret_norm = 1.0 + 0.02 * jax.random.normal(ks[11], (N_RET_LAYERS, D_MODEL), f32)
    w_in_ret = jax.random.normal(ks[12], (N_RET_LAYERS, D_MODEL, 2 * RET_QK_WIDTH + 2 * RET_V_WIDTH), f32) * D_MODEL ** -0.5
    ret_head_norm = 1.0 + 0.02 * jax.random.normal(ks[13], (N_RET_LAYERS, RET_V_DIM), f32)
    w_out_ret = jax.random.normal(ks[14], (N_RET_LAYERS, RET_V_WIDTH, D_MODEL), f32) * RET_V_WIDTH ** -0.5
    return {'x_prompt': x_prompt, 'x_sample': x_sample, 'cache_k': cache_k, 'cache_v': cache_v,
            'state_ret': state_ret, 'page_table': page_table, 'attn_norm': attn_norm,
            'w_in_attn': w_in_attn, 'q_norm': q_norm, 'k_norm': k_norm, 'w_out_attn': w_out_attn,
            'ret_norm': ret_norm, 'w_in_ret': w_in_ret, 'ret_head_norm': ret_head_norm,
            'w_out_ret': w_out_ret}


def reference(x_prompt, x_sample, cache_k, cache_v, state_ret, page_table, attn_norm, w_in_attn,
              q_norm, k_norm, w_out_attn, ret_norm, w_in_ret, ret_head_norm, w_out_ret):
    n_pages = page_table.shape[1]
    past_len = n_pages * PAGE_SIZE
    dec_b = x_sample.shape[0]
    xp, xs = x_prompt, x_sample
    kp_l, vp_l, ks_l, vs_l, sp_l, ss_l = [], [], [], [], [], []
    for layer in range(DEPTH):
        j = layer // N_MIXERS
        if layer % N_MIXERS == 0:
            xp, kp, vp = moba_layer(xp, 0, None, None, attn_norm[j], w_in_attn[j], q_norm[j],
                                    k_norm[j], w_out_attn[j])
            k_past = cache_k[j][page_table].reshape(dec_b, past_len, ATTN_HEADS, HEAD_DIM)
            v_past = cache_v[j][page_table].reshape(dec_b, past_len, ATTN_HEADS, HEAD_DIM)
            xs, kn_s, vn_s = moba_layer(xs, past_len, k_past, v_past, attn_norm[j], w_in_attn[j],
                                        q_norm[j], k_norm[j], w_out_attn[j])
            kp_l.append(kp)
            vp_l.append(vp)
            ks_l.append(kn_s)
            vs_l.append(vn_s)
        else:
            s0 = jnp.zeros((xp.shape[0], RET_HEADS, RET_QK_DIM, RET_V_DIM), jnp.float32)
            xp, sp = retention_layer(xp, 0, s0, ret_norm[j], w_in_ret[j], ret_head_norm[j], w_out_ret[j])
            xs, ss = retention_layer(xs, past_len, state_ret[j], ret_norm[j], w_in_ret[j],
                                     ret_head_norm[j], w_out_ret[j])
            sp_l.append(sp)
            ss_l.append(ss)
    return (xp, xs, jnp.stack(kp_l), jnp.stack(vp_l), jnp.stack(ks_l), jnp.stack(vs_l),
            jnp.stack(sp_l), jnp.stack(ss_l))
```

```python
import functools

import jax
import jax.numpy as jnp
from jax import lax
from jax.experimental import pallas as pl
from jax.experimental.pallas import tpu as pltpu

EPS = 1e-6
PAGE_SIZE = 128
MOBA_BLOCK = 256
MOBA_TOPK = 3
ROPE_THETA = 10000.0
RET_CHUNK = 128

F32 = jnp.float32
BF16 = jnp.bfloat16
NEG_INF = float("-inf")

LANES = 128
BF16_SUBLANES = 16
V7X_VMEM_BYTES = 64 * 1024 * 1024
VMEM_LIMIT_BYTES = V7X_VMEM_BYTES * 3 // 4

_NT = (((1,), (1,)), ((), ()))
_TN = (((0,), (0,)), ((), ()))


def _params(*sem):
    return pltpu.CompilerParams(dimension_semantics=sem, vmem_limit_bytes=VMEM_LIMIT_BYTES)


def _tile(n, target):
    if n <= target:
        return n
    t = target
    while n % t:
        t -= LANES if t > LANES else 1
    return t


def _silu(g):
    return g / (1.0 + jnp.exp(-g))


def _cast_kernel(x_ref, o_ref):
    o_ref[...] = x_ref[...].astype(o_ref.dtype)


def _to_bf16(w):
    k, n = w.shape
    bk, bn = _tile(k, 512), _tile(n, 2048)
    return pl.pallas_call(
        _cast_kernel,
        grid=(k // bk, n // bn),
        in_specs=[pl.BlockSpec((bk, bn), lambda i, j: (i, j))],
        out_specs=pl.BlockSpec((bk, bn), lambda i, j: (i, j)),
        out_shape=jax.ShapeDtypeStruct((k, n), BF16),
        compiler_params=_params("parallel", "parallel"),
        name="cast_bf16",
    )(w)


def _rmsnorm_kernel(x_ref, g_ref, o_ref):
    x = x_ref[...]
    ms = jnp.mean(x * x, axis=-1, keepdims=True)
    o_ref[...] = (x * lax.rsqrt(ms + EPS) * g_ref[...]).astype(o_ref.dtype)


def _rmsnorm_bf16(x, g):
    m, d = x.shape
    tm = _tile(m, 256)
    return pl.pallas_call(
        _rmsnorm_kernel,
        grid=(m // tm,),
        in_specs=[pl.BlockSpec((tm, d), lambda i: (i, 0)),
                  pl.BlockSpec((1, d), lambda i: (0, 0))],
        out_specs=pl.BlockSpec((tm, d), lambda i: (i, 0)),
        out_shape=jax.ShapeDtypeStruct((m, d), BF16),
        compiler_params=_params("parallel"),
        name="rmsnorm",
    )(x, g.reshape(1, d))


def _epi_plain(acc):
    return acc


def _epi_residual(acc, x_ref):
    return x_ref[...] + acc


def _epi_headnorm_rope(acc, gn_ref, cos_ref, sin_ref):
    hd = gn_ref.shape[-1]
    gn, cos2, sin2 = gn_ref[...], cos_ref[...], sin_ref[...]
    outs = []
    for c in range(acc.shape[1] // hd):
        z = acc[:, c * hd:(c + 1) * hd]
        ms = jnp.mean(z * z, axis=-1, keepdims=True)
        zn = z * lax.rsqrt(ms + EPS) * gn
        outs.append(zn * cos2 + pltpu.roll(zn, hd // 2, 1) * sin2)
    return jnp.concatenate(outs, axis=1)


def _epi_rope_wide(acc, cos_ref, sin_ref, *, scale):
    half = cos_ref.shape[-1]
    cos, sin = cos_ref[...], sin_ref[...]
    outs = []
    for c in range(acc.shape[1] // (2 * half)):
        x1 = acc[:, (2 * c) * half:(2 * c + 1) * half]
        x2 = acc[:, (2 * c + 1) * half:(2 * c + 2) * half]
        outs.append((x1 * cos - x2 * sin) * scale)
        outs.append((x2 * cos + x1 * sin) * scale)
    return jnp.concatenate(outs, axis=1)


def _proj_kernel(a_ref, w_ref, *refs, epilogue):
    *extra, o_ref = refs
    acc = jnp.dot(a_ref[...], w_ref[...], preferred_element_type=F32)
    o_ref[...] = epilogue(acc, *extra).astype(o_ref.dtype)


def _proj(a, w, col0, n_out, out_dtype, epilogue=_epi_plain, extras=(), name="proj"):
    m, k = a.shape
    tm = _tile(m, 1024 if k <= 4096 else 512)
    tn = _tile(n_out, 512)
    assert col0 % tn == 0
    off = col0 // tn
    in_specs = [pl.BlockSpec((tm, k), lambda i, j: (i, 0)),
                pl.BlockSpec((k, tn), lambda i, j: (0, j + off))]
    operands = [a, w]
    for arr, spec_fn in extras:
        in_specs.append(spec_fn(tm, tn))
        operands.append(arr)
    return pl.pallas_call(
        functools.partial(_proj_kernel, epilogue=epilogue),
        grid=(m // tm, n_out // tn),
        in_specs=in_specs,
        out_specs=pl.BlockSpec((tm, tn), lambda i, j: (i, j)),
        out_shape=jax.ShapeDtypeStruct((m, n_out), out_dtype),
        compiler_params=_params("parallel", "parallel"),
        name=name,
    )(*operands)


def _moba_prompt_kernel(q_ref, k_ref, v_ref, g_ref, o_ref, k16_ref, v16_ref, *, scale):
    seq, hd = q_ref.shape
    blk, topk = MOBA_BLOCK, MOBA_TOPK
    nb = seq // blk
    k16_ref[...] = k_ref[...].astype(BF16)
    v16_ref[...] = v_ref[...].astype(BF16)
    kmean_tiles = []
    for n in range(nb - 1):
        km = jnp.mean(k_ref[n * blk:(n + 1) * blk, :], axis=0, keepdims=True)
        kmean_tiles.append(jnp.broadcast_to(km, (hd, hd)).astype(BF16))
    row = lax.broadcasted_iota(jnp.int32, (blk, blk), 0)
    col = lax.broadcasted_iota(jnp.int32, (blk, blk), 1)
    causal_bias = jnp.where(col <= row, 0.0, NEG_INF).astype(F32)

    for j in range(nb):
        rows = slice(j * blk, (j + 1) * blk)
        qj = q_ref[rows, :]
        nk = (j + 1) * blk
        s = lax.dot_general(qj, k16_ref[0:nk, :], _NT, preferred_element_type=F32) * scale
        biases = None
        if j > topk:
            gates = [lax.dot_general(qj, kmean_tiles[n], _NT, preferred_element_type=F32)
                     for n in range(j)]
            biases = []
            for n in range(j):
                rank = jnp.zeros((blk, hd), F32)
                for m in range(j):
                    if m == n:
                        continue
                    ahead = (gates[m] >= gates[n]) if m < n else (gates[m] > gates[n])
                    rank = rank + jnp.where(ahead, 1.0, 0.0)
                b = jnp.where(rank < topk, 0.0, NEG_INF).astype(F32)
                biases.append(jnp.concatenate([b] * (blk // hd), axis=1))
        pieces = []
        for n in range(j + 1):
            sn = s[:, n * blk:(n + 1) * blk]
            if n == j:
                sn = sn + causal_bias
            elif biases is not None:
                sn = sn + biases[n]
            pieces.append(sn)
        mx = jnp.max(pieces[0], axis=1, keepdims=True)
        for sn in pieces[1:]:
            mx = jnp.maximum(mx, jnp.max(sn, axis=1, keepdims=True))
        probs = [jnp.exp(sn - mx) for sn in pieces]
        denom = jnp.sum(probs[0], axis=1, keepdims=True)
        for p in probs[1:]:
            denom = denom + jnp.sum(p, axis=1, keepdims=True)
        p16 = jnp.concatenate(probs, axis=1).astype(BF16)
        o = jnp.dot(p16, v16_ref[0:nk, :], preferred_element_type=F32) / denom
        o_ref[rows, :] = (o * _silu(g_ref[rows, :].astype(F32))).astype(o_ref.dtype)


def _moba_prompt(q, k, v, g, batch, seq, hd):
    m, width = q.shape
    heads = width // hd
    assert seq % MOBA_BLOCK == 0 and hd == LANES
    spec = pl.BlockSpec((seq, hd), lambda b, h: (b, h))
    return pl.pallas_call(
        functools.partial(_moba_prompt_kernel, scale=hd ** -0.5),
        grid=(batch, heads),
        in_specs=[spec, spec, spec, spec],
        out_specs=spec,
        out_shape=jax.ShapeDtypeStruct((m, width), BF16),
        scratch_shapes=[pltpu.VMEM((seq, hd), BF16), pltpu.VMEM((seq, hd), BF16)],
        compiler_params=_params("parallel", "parallel"),
        name="moba_prompt",
    )(q, k, v, g)


def _kmean_kernel(pt_ref, *refs, inv_count):
    *page_refs, o_ref = refs
    tot = jnp.sum(page_refs[0][...], axis=0, keepdims=True)
    for r in page_refs[1:]:
        tot = tot + jnp.sum(r[...], axis=0, keepdims=True)
    o_ref[...] = tot * inv_count


def _paged_block_means(cache, page_table):
    n_phys, page, width = cache.shape
    db, n_pages = page_table.shape
    ppb = MOBA_BLOCK // page
    assert n_pages % ppb == 0
    nblk = n_pages // ppb

    def page_spec(r):
        return pl.BlockSpec((None, page, width),
                            lambda b, n, pt: (pt[b * n_pages + n * ppb + r], 0, 0))

    return pl.pallas_call(
        functools.partial(_kmean_kernel, inv_count=1.0 / MOBA_BLOCK),
        grid_spec=pltpu.PrefetchScalarGridSpec(
            num_scalar_prefetch=1,
            grid=(db, nblk),
            in_specs=[page_spec(r) for r in range(ppb)],
            out_specs=pl.BlockSpec((None, None, 1, width), lambda b, n, pt: (b, n, 0, 0)),
        ),
        out_shape=jax.ShapeDtypeStruct((db, nblk, 1, width), F32),
        compiler_params=_params("parallel", "arbitrary"),
        name="paged_block_means",
    )(page_table.reshape(-1), *([cache] * ppb))


def _gate_topk_kernel(q_ref, km_ref, o_ref, *, heads, hd):
    q = q_ref[...].astype(BF16).astype(F32)
    km = km_ref[...].astype(BF16).astype(F32)
    prod = km * q
    nblk = km.shape[0]
    lane = lax.broadcasted_iota(jnp.int32, (nblk, LANES), 1)
    gate = jnp.full((nblk, LANES), NEG_INF, F32)
    for h in range(heads):
        gh = jnp.sum(prod[:, h * hd:(h + 1) * hd], axis=1, keepdims=True)
        gate = jnp.where(lane == h, gh, gate)
    rowf = lax.broadcasted_iota(jnp.int32, (nblk, LANES), 0).astype(F32)
    out_row = lax.broadcasted_iota(jnp.int32, o_ref.shape, 0)
    out = jnp.zeros(o_ref.shape, F32)
    for t in range(MOBA_TOPK):
        mx = jnp.max(gate, axis=0, keepdims=True)
        idx = jnp.min(jnp.where(gate == mx, rowf, float(nblk)), axis=0, keepdims=True)
        out = jnp.where(out_row == t, idx, out)
        gate = jnp.where(rowf == idx, NEG_INF, gate)
    o_ref[...] = out.astype(jnp.int32)


def _gate_topk(q, kmean, heads, hd):
    db, nblk, width = kmean.shape
    assert heads <= LANES and nblk >= MOBA_TOPK
    return pl.pallas_call(
        functools.partial(_gate_topk_kernel, heads=heads, hd=hd),
        grid=(db,),
        in_specs=[pl.BlockSpec((None, 1, width), lambda b: (b, 0, 0)),
                  pl.BlockSpec((None, nblk, width), lambda b: (b, 0, 0))],
        out_specs=pl.BlockSpec((None, 8, LANES), lambda b: (b, 0, 0)),
        out_shape=jax.ShapeDtypeStruct((db, 8, LANES), jnp.int32),
        compiler_params=_params("parallel"),
        name="gate_topk",
    )(q, kmean)


def _moba_decode_kernel(sel_ref, pt_ref, q_ref, kn_ref, vn_ref, g_ref, *refs, scale):
    *page_refs, o_ref = refs
    n = len(page_refs) // 2
    k_pages, v_pages = page_refs[:n], page_refs[n:]
    hd = q_ref.shape[-1]
    q = q_ref[...]
    q16 = jnp.broadcast_to(q, (BF16_SUBLANES, hd)).astype(BF16)
    scores = [lax.dot_general(q16, kp[...].astype(BF16), _NT,
                              preferred_element_type=F32)[0:1, :] * scale for kp in k_pages]
    qb = q.astype(BF16).astype(F32)
    s_new = jnp.sum(qb * kn_ref[...].astype(BF16).astype(F32), axis=1, keepdims=True) * scale
    mx = s_new
    for s in scores:
        mx = jnp.maximum(mx, jnp.max(s, axis=1, keepdims=True))
    p_new = jnp.exp(s_new - mx)
    denom = p_new
    acc = p_new.astype(BF16).astype(F32) * vn_ref[...].astype(BF16).astype(F32)
    for s, vp in zip(scores, v_pages):
        p = jnp.exp(s - mx)
        denom = denom + jnp.sum(p, axis=1, keepdims=True)
        p16 = jnp.broadcast_to(p, (BF16_SUBLANES, p.shape[1])).astype(BF16)
        acc = acc + jnp.dot(p16, vp[...].astype(BF16), preferred_element_type=F32)[0:1, :]
    o = acc / denom
    o_ref[...] = (o * _silu(g_ref[...])).astype(o_ref.dtype)


def _moba_decode(q, k_new, v_new, g, cache_k, cache_v, sel, page_table, heads, hd):
    db, _, width = q.shape
    n_pages = page_table.shape[1]
    page = cache_k.shape[1]
    ppb = MOBA_BLOCK // page

    row_spec = pl.BlockSpec((None, 1, hd), lambda b, h, sel, pt: (b, 0, h))

    def page_spec(t, r):
        def imap(b, h, sel, pt):
            blk = sel[(b * heads + h) * MOBA_TOPK + t]
            return (pt[b * n_pages + blk * ppb + r], 0, h)
        return pl.BlockSpec((None, page, hd), imap)

    page_specs = [page_spec(t, r) for t in range(MOBA_TOPK) for r in range(ppb)]
    n = len(page_specs)
    return pl.pallas_call(
        functools.partial(_moba_decode_kernel, scale=hd ** -0.5),
        grid_spec=pltpu.PrefetchScalarGridSpec(
            num_scalar_prefetch=2,
            grid=(db, heads),
            in_specs=[row_spec] * 4 + page_specs + page_specs,
            out_specs=row_spec,
        ),
        out_shape=jax.ShapeDtypeStruct((db, 1, width), BF16),
        compiler_params=_params("parallel", "parallel"),
        name="moba_decode",
    )(sel, page_table.reshape(-1), q, k_new, v_new, g, *([cache_k] * n), *([cache_v] * n))


def _head_norm_gate(o, hg, g):
    ms = jnp.mean(o * o, axis=-1, keepdims=True)
    return (o * lax.rsqrt(ms + EPS) * hg) * _silu(g)


def _ret_prompt_kernel(q_ref, k_ref, v_ref, g_ref, dm_ref, qd_ref, kd_ref, cd_ref, hg_ref,
                       o_ref, s_ref):
    c = dm_ref.shape[-1]
    n_chunks = q_ref.shape[0] // c
    s_ref[...] = jnp.zeros(s_ref.shape, F32)
    dm, qdec, kdec, cdec, hg = dm_ref[...], qd_ref[...], kd_ref[...], cd_ref[...], hg_ref[...]

    def chunk(ci, carry):
        r = pl.ds(pl.multiple_of(ci * c, c), c)
        qc, kc, vc = q_ref[r, :], k_ref[r, :], v_ref[r, :]
        s = s_ref[...]
        att = lax.dot_general(qc, kc, _NT, preferred_element_type=F32) * dm
        qd = (qc.astype(F32) * qdec).astype(BF16)
        o = (jnp.dot(att.astype(BF16), vc, preferred_element_type=F32)
             + jnp.dot(qd, s.astype(BF16), preferred_element_type=F32))
        kd = (kc.astype(F32) * kdec).astype(BF16)
        s_ref[...] = s * cdec + lax.dot_general(kd, vc, _TN, preferred_element_type=F32)
        o_ref[r, :] = _head_norm_gate(o, hg, g_ref[r, :].astype(F32)).astype(o_ref.dtype)
        return carry

    lax.fori_loop(0, n_chunks, chunk, 0)


def _ret_tables(heads, c, dk, dv):
    log_g = jnp.log1p(-jnp.exp2(-5.0 - jnp.arange(heads, dtype=F32)))
    i = jnp.arange(c, dtype=F32)
    diff = i[:, None] - i[None, :]
    dmask = jnp.where(diff >= 0, jnp.exp(log_g[:, None, None] * jnp.maximum(diff, 0.0)), 0.0)
    q_dec = jnp.exp(log_g[:, None] * (i[None, :] + 1.0))
    k_dec = jnp.exp(log_g[:, None] * (c - 1.0 - i[None, :]))
    chunk_dec = jnp.exp(log_g * c)
    return (dmask,
            jnp.broadcast_to(q_dec[:, :, None], (heads, c, dk)),
            jnp.broadcast_to(k_dec[:, :, None], (heads, c, dk)),
            jnp.broadcast_to(chunk_dec[:, None, None], (heads, 1, dv)))


def _ret_prompt(q, k, v, g, head_g, batch, seq, heads):
    m = q.shape[0]
    dk, dv = q.shape[1] // heads, v.shape[1] // heads
    c = RET_CHUNK
    assert seq % c == 0
    dmask, qdec, kdec, cdec = _ret_tables(heads, c, dk, dv)
    qk_spec = pl.BlockSpec((seq, dk), lambda b, h: (b, h))
    v_spec = pl.BlockSpec((seq, dv), lambda b, h: (b, h))

    def head_spec(*shape):
        return pl.BlockSpec((None,) + shape, lambda b, h: (h, 0, 0))

    return pl.pallas_call(
        _ret_prompt_kernel,
        grid=(batch, heads),
        in_specs=[qk_spec, qk_spec, v_spec, v_spec, head_spec(c, c), head_spec(c, dk),
                  head_spec(c, dk), head_spec(1, dv), pl.BlockSpec((1, dv), lambda b, h: (0, 0))],
        out_specs=[v_spec, pl.BlockSpec((None, None, dk, dv), lambda b, h: (b, h, 0, 0))],
        out_shape=[jax.ShapeDtypeStruct((m, heads * dv), BF16),
                   jax.ShapeDtypeStruct((batch, heads, dk, dv), F32)],
        compiler_params=_params("parallel", "parallel"),
        name="ret_prompt",
    )(q, k, v, g, dmask, qdec, kdec, cdec, head_g.reshape(1, dv))


def _ret_decode_kernel(q_ref, k_ref, kcol_ref, v_ref, g_ref, s0_ref, gam_ref, hg_ref,
                       o_ref, s_ref):
    q, k, v, s0, gam = q_ref[...], k_ref[...], v_ref[...], s0_ref[...], gam_ref[...]
    dk = q.shape[-1]
    att = jnp.sum(q.astype(BF16).astype(F32) * k.astype(BF16).astype(F32), axis=1, keepdims=True)
    qd = jnp.broadcast_to(q * gam[:, :dk], (BF16_SUBLANES, dk)).astype(BF16)
    o = att * v + jnp.dot(qd, s0.astype(BF16), preferred_element_type=F32)[0:1, :]
    s_ref[...] = s0 * gam + kcol_ref[...] * v
    o_ref[...] = _head_norm_gate(o, hg_ref[...], g_ref[...]).astype(o_ref.dtype)


def _ret_decode(q, k, v, g, s0, head_g, heads):
    db = q.shape[0]
    dk, dv = q.shape[2] // heads, v.shape[2] // heads
    assert dv >= dk
    gamma = jnp.exp(jnp.log1p(-jnp.exp2(-5.0 - jnp.arange(heads, dtype=F32))) * 1.0)
    gam = jnp.broadcast_to(gamma[:, None, None], (heads, 1, dv))
    kcol = k.reshape(db, heads, dk, 1)
    qk_spec = pl.BlockSpec((None, 1, dk), lambda b, h: (b, 0, h))
    v_spec = pl.BlockSpec((None, 1, dv), lambda b, h: (b, 0, h))
    s_spec = pl.BlockSpec((None, None, dk, dv), lambda b, h: (b, h, 0, 0))
    return pl.pallas_call(
        _ret_decode_kernel,
        grid=(db, heads),
        in_specs=[qk_spec, qk_spec,
                  pl.BlockSpec((None, None, dk, 1), lambda b, h: (b, h, 0, 0)),
                  v_spec, v_spec, s_spec,
                  pl.BlockSpec((None, 1, dv), lambda b, h: (h, 0, 0)),
                  pl.BlockSpec((1, dv), lambda b, h: (0, 0))],
        out_specs=[v_spec, s_spec],
        out_shape=[jax.ShapeDtypeStruct((db, 1, heads * dv), BF16),
                   jax.ShapeDtypeStruct(s0.shape, F32)],
        compiler_params=_params("parallel", "parallel"),
        name="ret_decode",
    )(q, k, kcol, v, g, s0, gam, head_g.reshape(1, dv))


def _rope_tables(pos, dim):
    half = dim // 2
    inv = ROPE_THETA ** (-jnp.arange(half, dtype=F32) / half)
    ang = pos.astype(F32)[:, None] * inv[None, :]
    return jnp.cos(ang), jnp.sin(ang)


def _row_tables(tabs, seq_rows):
    def spec_fn(tab):
        def make(tm, tn):
            assert seq_rows % tm == 0
            return pl.BlockSpec((tm, tab.shape[1]), lambda i, j: (i % (seq_rows // tm), 0))
        return make
    return [(tab, spec_fn(tab)) for tab in tabs]


def _const_extra(arr):
    return arr, lambda tm, tn: pl.BlockSpec(arr.shape, lambda i, j: (0,) * arr.ndim)


def _moba_in_proj(h, w, qn, kn, pos, seq_rows, width, hd, q_dtype, g_dtype):
    cos, sin = _rope_tables(pos, hd)
    cos2 = jnp.concatenate([cos, cos], axis=-1)
    sin2 = jnp.concatenate([-sin, sin], axis=-1)
    tabs = _row_tables([cos2, sin2], seq_rows)
    q = _proj(h, w, 0, width, q_dtype, _epi_headnorm_rope,
              [_const_extra(qn.reshape(1, hd))] + tabs, name="attn_q_proj")
    k = _proj(h, w, width, width, F32, _epi_headnorm_rope,
              [_const_extra(kn.reshape(1, hd))] + tabs, name="attn_k_proj")
    v = _proj(h, w, 2 * width, width, F32, name="attn_v_proj")
    g = _proj(h, w, 3 * width, width, g_dtype, name="attn_g_proj")
    return q, k, v, g


def _ret_in_proj(h, w, pos, seq_rows, qk_width, v_width, dk, dtype):
    cos, sin = _rope_tables(pos, dk)
    tabs = _row_tables([cos, sin], seq_rows)
    q = _proj(h, w, 0, qk_width, dtype, functools.partial(_epi_rope_wide, scale=1.0), tabs,
              name="ret_q_proj")
    k = _proj(h, w, qk_width, qk_width, dtype,
              functools.partial(_epi_rope_wide, scale=dk ** -0.5), tabs, name="ret_k_proj")
    v = _proj(h, w, 2 * qk_width, v_width, dtype, name="ret_v_proj")
    g = _proj(h, w, 2 * qk_width + v_width, v_width, dtype, name="ret_g_proj")
    return q, k, v, g


def _residual_proj(a, w, x, name):
    extra = (x, lambda tm, tn: pl.BlockSpec((tm, tn), lambda i, j: (i, j)))
    return _proj(a, w, 0, w.shape[1], F32, _epi_residual, [extra], name=name)


def kernel(x_prompt, x_sample, cache_k, cache_v, state_ret, page_table, attn_norm, w_in_attn,
           q_norm, k_norm, w_out_attn, ret_norm, w_in_ret, ret_head_norm, w_out_ret):
    batch, seq, d = x_prompt.shape
    db, dec_seq, _ = x_sample.shape
    assert dec_seq == 1
    n_attn, n_ret = attn_norm.shape[0], ret_norm.shape[0]
    hd = q_norm.shape[-1]
    width = w_out_attn.shape[1]
    heads = width // hd
    dv = ret_head_norm.shape[-1]
    v_width = w_out_ret.shape[1]
    ret_heads = v_width // dv
    qk_width = (w_in_ret.shape[-1] - 2 * v_width) // 2
    dk = qk_width // ret_heads
    assert dk == 2 * LANES
    n_pages = page_table.shape[1]
    past_len = n_pages * cache_k.shape[2]
    assert cache_k.shape[2] == PAGE_SIZE and past_len % MOBA_BLOCK == 0

    m = batch * seq
    sp = BF16_SUBLANES * (-(-db // BF16_SUBLANES))
    xp = x_prompt.reshape(m, d)
    xs = jnp.pad(x_sample.reshape(db, d), ((0, sp - db), (0, 0)))
    pos_p = jnp.arange(seq, dtype=jnp.int32)
    pos_s = jnp.full((sp,), past_len, jnp.int32)

    kp_l, vp_l, ks_l, vs_l, sp_l, ss_l = [], [], [], [], [], []
    for layer in range(n_attn + n_ret):
        j = layer // 2
        if layer % 2 == 0:
            w_in = _to_bf16(w_in_attn[j])
            w_out = _to_bf16(w_out_attn[j])
            h = _rmsnorm_bf16(xp, attn_norm[j])
            q, k, v, g = _moba_in_proj(h, w_in, q_norm[j], k_norm[j], pos_p, seq, width, hd,
                                       BF16, BF16)
            o = _moba_prompt(q, k, v, g, batch, seq, hd)
            xp = _residual_proj(o, w_out, xp, "attn_out_proj")
            kp_l.append(k.reshape(batch, seq, heads, hd))
            vp_l.append(v.reshape(batch, seq, heads, hd))
            h = _rmsnorm_bf16(xs, attn_norm[j])
            q, k, v, g = _moba_in_proj(h, w_in, q_norm[j], k_norm[j], pos_s, sp, width, hd,
                                       F32, F32)
            q, k, v, g = (t[:db].reshape(db, 1, width) for t in (q, k, v, g))
            ck = cache_k[j].reshape(cache_k.shape[1], PAGE_SIZE, width)
            cv = cache_v[j].reshape(cache_v.shape[1], PAGE_SIZE, width)
            kmean = _paged_block_means(ck, page_table).reshape(db, -1, width)
            sel = _gate_topk(q, kmean, heads, hd)[:, :MOBA_TOPK, :heads]
            sel = jnp.swapaxes(sel, 1, 2).reshape(-1)
            o = _moba_decode(q, k, v, g, ck, cv, sel, page_table, heads, hd)
            o = jnp.pad(o.reshape(db, width), ((0, sp - db), (0, 0)))
            xs = _residual_proj(o, w_out, xs, "attn_out_proj_s")
            ks_l.append(k.reshape(db, 1, heads, hd))
            vs_l.append(v.reshape(db, 1, heads, hd))
        else:
            w_in = _to_bf16(w_in_ret[j])
            w_out = _to_bf16(w_out_ret[j])
            h = _rmsnorm_bf16(xp, ret_norm[j])
            q, k, v, g = _ret_in_proj(h, w_in, pos_p, seq, qk_width, v_width, dk, BF16)
            o, s = _ret_prompt(q, k, v, g, ret_head_norm[j], batch, seq, ret_heads)
            xp = _residual_proj(o, w_out, xp, "ret_out_proj")
            sp_l.append(s)
            h = _rmsnorm_bf16(xs, ret_norm[j])
            q, k, v, g = _ret_in_proj(h, w_in, pos_s, sp, qk_width, v_width, dk, F32)
            q, k, v, g = (t[:db].reshape(db, 1, -1) for t in (q, k, v, g))
            o, s = _ret_decode(q, k, v, g, state_ret[j], ret_head_norm[j], ret_heads)
            o = jnp.pad(o.reshape(db, v_width), ((0, sp - db), (0, 0)))
            xs = _residual_proj(o, w_out, xs, "ret_out_proj_s")
            ss_l.append(s)

    return (xp.reshape(batch, seq, d), xs[:db].reshape(db, 1, d),
            jnp.stack(kp_l), jnp.stack(vp_l), jnp.stack(ks_l), jnp.stack(vs_l),
            jnp.stack(sp_l), jnp.stack(ss_l))
```

```python
import functools
import math

import jax
import jax.numpy as jnp
from jax import lax
from jax.experimental import pallas as pl
from jax.experimental.pallas import tpu as pltpu

EPS = 1e-6
PAGE_SIZE = 128
MOBA_BLOCK = 256
MOBA_TOPK = 3
ROPE_THETA = 10000.0
RET_CHUNK = 128

F32 = jnp.float32
BF16 = jnp.bfloat16
NEG_INF = float("-inf")

LANES = 128
SUBLANES = 8
BF16_SUBLANES = 16
V7X_VMEM_BYTES = 64 * 1024 * 1024
VMEM_LIMIT_BYTES = V7X_VMEM_BYTES * 7 // 8

PROJ_TM = 1024
PROJ_TN = 512
PROJ_TK = 4096
PROJ_ROW_CHUNK = 256

_NT = (((1,), (1,)), ((), ()))
_TN = (((0,), (0,)), ((), ()))


def _params(*sem):
    return pltpu.CompilerParams(dimension_semantics=sem, vmem_limit_bytes=VMEM_LIMIT_BYTES)


def _tile(n, target):
    if n <= target:
        return n
    t = target
    while n % t:
        t -= LANES if t > LANES else 1
    return t


def _silu(g):
    return g / (1.0 + jnp.exp(-g))


def _rmsnorm_kernel(x_ref, g_ref, o_ref):
    x = x_ref[...]
    ms = jnp.mean(x * x, axis=-1, keepdims=True)
    o_ref[...] = (x * lax.rsqrt(ms + EPS) * g_ref[...]).astype(o_ref.dtype)


def _rmsnorm_bf16(x, g):
    m, d = x.shape
    tm = _tile(m, 256)
    return pl.pallas_call(
        _rmsnorm_kernel,
        grid=(m // tm,),
        in_specs=[pl.BlockSpec((tm, d), lambda i: (i, 0)),
                  pl.BlockSpec((1, d), lambda i: (0, 0))],
        out_specs=pl.BlockSpec((tm, d), lambda i: (i, 0)),
        out_shape=jax.ShapeDtypeStruct((m, d), BF16),
        compiler_params=_params("parallel"),
        name="rmsnorm",
    )(x, g.reshape(1, d))


def _epi_plain(acc, rows):
    return acc


def _epi_residual(acc, rows, x_ref):
    return x_ref[rows, :] + acc


def _epi_headnorm_rope(acc, rows, gn_ref, cos_ref, sin_ref):
    hd = gn_ref.shape[-1]
    gn, cos2, sin2 = gn_ref[...], cos_ref[rows, :], sin_ref[rows, :]
    outs = []
    for c in range(acc.shape[1] // hd):
        z = acc[:, c * hd:(c + 1) * hd]
        ms = jnp.mean(z * z, axis=-1, keepdims=True)
        zn = z * lax.rsqrt(ms + EPS) * gn
        outs.append(zn * cos2 + pltpu.roll(zn, hd // 2, 1) * sin2)
    return jnp.concatenate(outs, axis=1)


def _epi_rope_wide(acc, rows, cos_ref, sin_ref, *, scale):
    half = cos_ref.shape[-1]
    cos, sin = cos_ref[rows, :], sin_ref[rows, :]
    outs = []
    for c in range(acc.shape[1] // (2 * half)):
        x1 = acc[:, (2 * c) * half:(2 * c + 1) * half]
        x2 = acc[:, (2 * c + 1) * half:(2 * c + 2) * half]
        outs.append((x1 * cos - x2 * sin) * scale)
        outs.append((x2 * cos + x1 * sin) * scale)
    return jnp.concatenate(outs, axis=1)


def _proj_kernel(*refs, epilogue, n_extra, nk, row_chunk):
    ap_ref, as_ref, w_ref = refs[:3]
    extras_p = refs[3:3 + n_extra]
    extras_s = refs[3 + n_extra:3 + 2 * n_extra]
    op_ref, os_ref, wbf_ref = refs[3 + 2 * n_extra:6 + 2 * n_extra]
    acc_refs = refs[6 + 2 * n_extra:]
    i = pl.program_id(1)
    kk = pl.program_id(2) if nk > 1 else 0
    tm = ap_ref.shape[0]
    all_rows = slice(None)

    def tile(a_ref, acc_ref, o_ref, extras, row_slices):
        for rows in row_slices:
            acc = jnp.dot(a_ref[rows, :], wbf_ref[kk], preferred_element_type=F32)
            if nk == 1:
                o_ref[rows, :] = epilogue(acc, rows, *extras).astype(o_ref.dtype)
                continue

            @pl.when(kk == 0)
            def _():
                acc_ref[rows, :] = acc

            @pl.when(jnp.logical_and(kk > 0, kk < nk - 1))
            def _():
                acc_ref[rows, :] += acc

            @pl.when(kk == nk - 1)
            def _():
                o_ref[rows, :] = epilogue(acc_ref[rows, :] + acc, rows, *extras).astype(o_ref.dtype)

    @pl.when(i == 0)
    def _():
        wbf_ref[kk] = w_ref[...].astype(BF16)
        tile(as_ref, acc_refs[1] if nk > 1 else None, os_ref, extras_s, [all_rows])

    tile(ap_ref, acc_refs[0] if nk > 1 else None, op_ref, extras_p,
         [slice(r, r + row_chunk) for r in range(0, tm, row_chunk)])


def _proj(a_p, a_s, w, col0, n_out, dtype_p, dtype_s, epilogue=_epi_plain, extras=(), name="proj"):
    m, k = a_p.shape
    s = a_s.shape[0]
    tm, tn, tk = _tile(m, PROJ_TM), _tile(n_out, PROJ_TN), _tile(k, PROJ_TK)
    nk = k // tk
    row_chunk = _tile(tm, PROJ_ROW_CHUNK)
    assert col0 % tn == 0
    off = col0 // tn

    def extra_specs(kind, arr, rows_per_tile, is_prompt):
        if kind == "const":
            return pl.BlockSpec(arr.shape, lambda j, i, kk: (0,) * arr.ndim)
        if kind == "rows":
            assert arr.shape[0] % rows_per_tile == 0
            period = arr.shape[0] // rows_per_tile
            if is_prompt:
                return pl.BlockSpec((rows_per_tile, arr.shape[1]), lambda j, i, kk: (i % period, 0))
            return pl.BlockSpec((rows_per_tile, arr.shape[1]), lambda j, i, kk: (0, 0))
        assert kind == "tile"
        if is_prompt:
            return pl.BlockSpec((rows_per_tile, tn), lambda j, i, kk: (i, j))
        return pl.BlockSpec((rows_per_tile, tn), lambda j, i, kk: (0, j))

    in_specs = [pl.BlockSpec((tm, tk), lambda j, i, kk: (i, kk)),
                pl.BlockSpec((s, tk), lambda j, i, kk: (0, kk)),
                pl.BlockSpec((tk, tn), lambda j, i, kk: (kk, j + off))]
    in_specs += [extra_specs(kind, ap, tm, True) for kind, ap, _ in extras]
    in_specs += [extra_specs(kind, asm, s, False) for kind, _, asm in extras]
    operands = [a_p, a_s, w] + [ap for _, ap, _ in extras] + [asm for _, _, asm in extras]
    scratch = [pltpu.VMEM((nk, tk, tn), BF16)]
    if nk > 1:
        scratch += [pltpu.VMEM((tm, tn), F32), pltpu.VMEM((s, tn), F32)]
    return pl.pallas_call(
        functools.partial(_proj_kernel, epilogue=epilogue, n_extra=len(extras), nk=nk,
                          row_chunk=row_chunk),
        grid=(n_out // tn, m // tm, nk),
        in_specs=in_specs,
        out_specs=[pl.BlockSpec((tm, tn), lambda j, i, kk: (i, j)),
                   pl.BlockSpec((s, tn), lambda j, i, kk: (0, j))],
        out_shape=[jax.ShapeDtypeStruct((m, n_out), dtype_p),
                   jax.ShapeDtypeStruct((s, n_out), dtype_s)],
        scratch_shapes=scratch,
        compiler_params=_params("parallel", "arbitrary", "arbitrary"),
        name=name,
    )(*operands)


def _moba_prompt_kernel(q_ref, k_ref, v_ref, g_ref, o_ref, k16_ref, vt16_ref, *, scale):
    seq, hd = q_ref.shape
    blk, topk = MOBA_BLOCK, MOBA_TOPK
    nb = seq // blk
    k16_ref[...] = k_ref[...].astype(BF16)
    vt16_ref[...] = v_ref[...].T.astype(BF16)
    km_rows = SUBLANES * max(1, -(-(nb - 1) // SUBLANES))
    row_id = lax.broadcasted_iota(jnp.int32, (km_rows, hd), 0)
    kmean = jnp.zeros((km_rows, hd), F32)
    for n in range(nb - 1):
        km = jnp.mean(k_ref[n * blk:(n + 1) * blk, :], axis=0, keepdims=True)
        kmean = jnp.where(row_id == n, km, kmean)
    kmean16 = kmean.astype(BF16)
    key_id = lax.broadcasted_iota(jnp.int32, (blk, blk), 0)
    qry_id = lax.broadcasted_iota(jnp.int32, (blk, blk), 1)
    causal_bias = jnp.where(key_id <= qry_id, 0.0, NEG_INF).astype(F32)
    c = scale * math.log2(math.e)

    def scores(j):
        qj = q_ref[j * blk:(j + 1) * blk, :]
        st = lax.dot_general(k16_ref[0:(j + 1) * blk, :], qj, _NT,
                             preferred_element_type=F32)
        gt = None
        if j > topk:
            gt = lax.dot_general(kmean16, qj, _NT, preferred_element_type=F32)
        return st, gt

    def softmax(j, st, gt):
        biases = None
        if j > topk:
            gates = [gt[n:n + 1, :] for n in range(j)]
            ranks = [jnp.zeros((1, blk), F32) for _ in range(j)]
            for lo in range(j):
                for hi in range(lo + 1, j):
                    lo_ahead = jnp.where(gates[lo] >= gates[hi], 1.0, 0.0)
                    ranks[hi] = ranks[hi] + lo_ahead
                    ranks[lo] = ranks[lo] + (1.0 - lo_ahead)
            biases = [jnp.where(r < topk, 0.0, NEG_INF).astype(F32) for r in ranks]
        pieces = []
        for n in range(j + 1):
            sn = st[n * blk:(n + 1) * blk, :]
            if n == j:
                sn = sn + causal_bias
            elif biases is not None:
                sn = sn + biases[n]
            pieces.append(sn)
        mx = jnp.max(pieces[0], axis=0, keepdims=True)
        for sn in pieces[1:]:
            mx = jnp.maximum(mx, jnp.max(sn, axis=0, keepdims=True))
        probs = [jnp.exp2((sn - mx) * c) for sn in pieces]
        denom = jnp.sum(probs[0], axis=0, keepdims=True)
        for p in probs[1:]:
            denom = denom + jnp.sum(p, axis=0, keepdims=True)
        return jnp.concatenate(probs, axis=0).astype(BF16), denom

    def finish(j, pt16, denom):
        rows = slice(j * blk, (j + 1) * blk)
        ot = jnp.dot(vt16_ref[:, 0:(j + 1) * blk], pt16, preferred_element_type=F32) / denom
        o_ref[rows, :] = (ot.T * _silu(g_ref[rows, :].astype(F32))).astype(o_ref.dtype)

    nxt = scores(0)
    pending = None
    for j in range(nb):
        cur = nxt
        if j + 1 < nb:
            nxt = scores(j + 1)
        if pending is not None:
            finish(*pending)
        pending = (j,) + softmax(j, *cur)
    finish(*pending)


def _moba_prompt(q, k, v, g, batch, seq, hd):
    m, width = q.shape
    heads = width // hd
    assert seq % MOBA_BLOCK == 0 and hd == LANES
    spec = pl.BlockSpec((seq, hd), lambda b, h: (b, h))
    return pl.pallas_call(
        functools.partial(_moba_prompt_kernel, scale=hd ** -0.5),
        grid=(batch, heads),
        in_specs=[spec, spec, spec, spec],
        out_specs=spec,
        out_shape=jax.ShapeDtypeStruct((m, width), BF16),
        scratch_shapes=[pltpu.VMEM((seq, hd), BF16), pltpu.VMEM((hd, seq), BF16)],
        compiler_params=_params("parallel", "parallel"),
        name="moba_prompt",
    )(q, k, v, g)


def _kmean_kernel(pt_ref, *refs, inv_count):
    *page_refs, o_ref = refs
    tot = jnp.sum(page_refs[0][...], axis=0)
    for r in page_refs[1:]:
        tot = tot + jnp.sum(r[...], axis=0)
    o_ref[...] = tot * inv_count


def _paged_block_means(cache, layer, page_table):
    _, _, page, heads, hd = cache.shape
    db, n_pages = page_table.shape
    ppb = MOBA_BLOCK // page
    assert n_pages % ppb == 0
    nblk = n_pages // ppb

    def page_spec(r):
        return pl.BlockSpec((None, None, page, heads, hd),
                            lambda b, n, pt: (layer, pt[b * n_pages + n * ppb + r], 0, 0, 0))

    return pl.pallas_call(
        functools.partial(_kmean_kernel, inv_count=1.0 / MOBA_BLOCK),
        grid_spec=pltpu.PrefetchScalarGridSpec(
            num_scalar_prefetch=1,
            grid=(db, nblk),
            in_specs=[page_spec(r) for r in range(ppb)],
            out_specs=pl.BlockSpec((None, None, heads, hd), lambda b, n, pt: (b, n, 0, 0)),
        ),
        out_shape=jax.ShapeDtypeStruct((db, nblk, heads, hd), F32),
        compiler_params=_params("parallel", "arbitrary"),
        name="paged_block_means",
    )(page_table.reshape(-1), *([cache] * ppb))


def _gate_topk_kernel(q_ref, km_ref, o_ref, *, heads, hd):
    q = q_ref[...].astype(BF16).astype(F32)
    km = km_ref[...].astype(BF16).astype(F32)
    prod = km * q
    nblk = km.shape[0]
    lane = lax.broadcasted_iota(jnp.int32, (nblk, LANES), 1)
    gate = jnp.full((nblk, LANES), NEG_INF, F32)
    for h in range(heads):
        gh = jnp.sum(prod[:, h * hd:(h + 1) * hd], axis=1, keepdims=True)
        gate = jnp.where(lane == h, gh, gate)
    rowf = lax.broadcasted_iota(jnp.int32, (nblk, LANES), 0).astype(F32)
    out_row = lax.broadcasted_iota(jnp.int32, o_ref.shape, 0)
    out = jnp.zeros(o_ref.shape, F32)
    for t in range(MOBA_TOPK):
        mx = jnp.max(gate, axis=0, keepdims=True)
        idx = jnp.min(jnp.where(gate == mx, rowf, float(nblk)), axis=0, keepdims=True)
        out = jnp.where(out_row == t, idx, out)
        gate = jnp.where(rowf == idx, NEG_INF, gate)
    o_ref[...] = out.astype(jnp.int32)


def _gate_topk(q, kmean, heads, hd):
    db, nblk, width = kmean.shape
    assert heads <= LANES and nblk >= MOBA_TOPK
    return pl.pallas_call(
        functools.partial(_gate_topk_kernel, heads=heads, hd=hd),
        grid=(db,),
        in_specs=[pl.BlockSpec((None, 1, width), lambda b: (b, 0, 0)),
                  pl.BlockSpec((None, nblk, width), lambda b: (b, 0, 0))],
        out_specs=pl.BlockSpec((None, SUBLANES, LANES), lambda b: (b, 0, 0)),
        out_shape=jax.ShapeDtypeStruct((db, SUBLANES, LANES), jnp.int32),
        compiler_params=_params("parallel"),
        name="gate_topk",
    )(q, kmean)


def _moba_decode_kernel(sel_ref, pt_ref, q_ref, kn_ref, vn_ref, g_ref, *refs, scale):
    *page_refs, o_ref = refs
    n = len(page_refs) // 2
    k_pages, v_pages = page_refs[:n], page_refs[n:]
    page, group, hd = k_pages[0].shape
    head_in_group = pl.program_id(1) % group
    row_head = lax.broadcasted_iota(jnp.int32, (1, page * group), 1) % group
    head_bias = jnp.where(row_head == head_in_group, 0.0, NEG_INF).astype(F32)

    def flat16(ref):
        return ref[...].reshape(page * group, hd).astype(BF16)

    q = q_ref[...]
    q16 = jnp.broadcast_to(q, (BF16_SUBLANES, hd)).astype(BF16)
    scores = [lax.dot_general(q16, flat16(kp), _NT, preferred_element_type=F32)[0:1, :] * scale
              + head_bias for kp in k_pages]
    qb = q.astype(BF16).astype(F32)
    s_new = jnp.sum(qb * kn_ref[...].astype(BF16).astype(F32), axis=1, keepdims=True) * scale
    mx = s_new
    for s in scores:
        mx = jnp.maximum(mx, jnp.max(s, axis=1, keepdims=True))
    p_new = jnp.exp(s_new - mx)
    denom = p_new
    acc = p_new.astype(BF16).astype(F32) * vn_ref[...].astype(BF16).astype(F32)
    for s, vp in zip(scores, v_pages):
        p = jnp.exp(s - mx)
        denom = denom + jnp.sum(p, axis=1, keepdims=True)
        p16 = jnp.broadcast_to(p, (BF16_SUBLANES, p.shape[1])).astype(BF16)
        acc = acc + jnp.dot(p16, flat16(vp), preferred_element_type=F32)[0:1, :]
    o = acc / denom
    o_ref[...] = (o * _silu(g_ref[...])).astype(o_ref.dtype)


def _moba_decode(q, k_new, v_new, g, cache_k, cache_v, layer, sel, page_table):
    db, _, width = q.shape
    _, _, page, heads, hd = cache_k.shape
    n_pages = page_table.shape[1]
    ppb = MOBA_BLOCK // page
    group = min(heads, SUBLANES)
    assert heads % group == 0

    row_spec = pl.BlockSpec((None, 1, hd), lambda b, h, sel, pt: (b, 0, h))

    def page_spec(t, r):
        def imap(b, h, sel, pt):
            blk = sel[(b * heads + h) * MOBA_TOPK + t]
            return (layer, pt[b * n_pages + blk * ppb + r], 0, h // group, 0)
        return pl.BlockSpec((None, None, page, group, hd), imap)

    page_specs = [page_spec(t, r) for t in range(MOBA_TOPK) for r in range(ppb)]
    n = len(page_specs)
    return pl.pallas_call(
        functools.partial(_moba_decode_kernel, scale=hd ** -0.5),
        grid_spec=pltpu.PrefetchScalarGridSpec(
            num_scalar_prefetch=2,
            grid=(db, heads),
            in_specs=[row_spec] * 4 + page_specs + page_specs,
            out_specs=row_spec,
        ),
        out_shape=jax.ShapeDtypeStruct((db, 1, width), BF16),
        compiler_params=_params("parallel", "parallel"),
        name="moba_decode",
    )(sel, page_table.reshape(-1), q, k_new, v_new, g, *([cache_k] * n), *([cache_v] * n))


def _head_norm_gate(o, hg, g):
    ms = jnp.mean(o * o, axis=-1, keepdims=True)
    return (o * lax.rsqrt(ms + EPS) * hg) * _silu(g)


def _ret_prompt_kernel(q_ref, k_ref, v_ref, g_ref, dm_ref, qd_ref, kd_ref, cd_ref, hg_ref,
                       o_ref, s_ref):
    c = dm_ref.shape[-1]
    n_chunks = q_ref.shape[0] // c
    s_ref[...] = jnp.zeros(s_ref.shape, F32)
    dm, qdec, kdec, cdec, hg = dm_ref[...], qd_ref[...], kd_ref[...], cd_ref[...], hg_ref[...]

    def chunk(ci, carry):
        r = pl.ds(pl.multiple_of(ci * c, c), c)
        qc, kc, vc = q_ref[r, :], k_ref[r, :], v_ref[r, :]
        s = s_ref[...]
        att = lax.dot_general(qc, kc, _NT, preferred_element_type=F32) * dm
        qd = (qc.astype(F32) * qdec).astype(BF16)
        o = (jnp.dot(att.astype(BF16), vc, preferred_element_type=F32)
             + jnp.dot(qd, s.astype(BF16), preferred_element_type=F32))
        kd = (kc.astype(F32) * kdec).astype(BF16)
        s_ref[...] = s * cdec + lax.dot_general(kd, vc, _TN, preferred_element_type=F32)
        o_ref[r, :] = _head_norm_gate(o, hg, g_ref[r, :].astype(F32)).astype(o_ref.dtype)
        return carry

    lax.fori_loop(0, n_chunks, chunk, 0)


def _ret_tables(heads, c, dk, dv):
    log_g = jnp.log1p(-jnp.exp2(-5.0 - jnp.arange(heads, dtype=F32)))
    i = jnp.arange(c, dtype=F32)
    diff = i[:, None] - i[None, :]
    dmask = jnp.where(diff >= 0, jnp.exp(log_g[:, None, None] * jnp.maximum(diff, 0.0)), 0.0)
    q_dec = jnp.exp(log_g[:, None] * (i[None, :] + 1.0))
    k_dec = jnp.exp(log_g[:, None] * (c - 1.0 - i[None, :]))
    chunk_dec = jnp.exp(log_g * c)
    return (dmask,
            jnp.broadcast_to(q_dec[:, :, None], (heads, c, dk)),
            jnp.broadcast_to(k_dec[:, :, None], (heads, c, dk)),
            jnp.broadcast_to(chunk_dec[:, None, None], (heads, 1, dv)))


def _ret_prompt(q, k, v, g, head_g, batch, seq, heads):
    m = q.shape[0]
    dk, dv = q.shape[1] // heads, v.shape[1] // heads
    c = RET_CHUNK
    assert seq % c == 0
    dmask, qdec, kdec, cdec = _ret_tables(heads, c, dk, dv)
    qk_spec = pl.BlockSpec((seq, dk), lambda b, h: (b, h))
    v_spec = pl.BlockSpec((seq, dv), lambda b, h: (b, h))

    def head_spec(*shape):
        return pl.BlockSpec((None,) + shape, lambda b, h: (h, 0, 0))

    return pl.pallas_call(
        _ret_prompt_kernel,
        grid=(batch, heads),
        in_specs=[qk_spec, qk_spec, v_spec, v_spec, head_spec(c, c), head_spec(c, dk),
                  head_spec(c, dk), head_spec(1, dv), pl.BlockSpec((1, dv), lambda b, h: (0, 0))],
        out_specs=[v_spec, pl.BlockSpec((None, None, dk, dv), lambda b, h: (b, h, 0, 0))],
        out_shape=[jax.ShapeDtypeStruct((m, heads * dv), BF16),
                   jax.ShapeDtypeStruct((batch, heads, dk, dv), F32)],
        compiler_params=_params("parallel", "parallel"),
        name="ret_prompt",
    )(q, k, v, g, dmask, qdec, kdec, cdec, head_g.reshape(1, dv))


def _ret_decode_kernel(q_ref, k_ref, kcol_ref, v_ref, g_ref, s0_ref, gam_ref, hg_ref,
                       o_ref, s_ref):
    q, k, v, s0, gam = q_ref[...], k_ref[...], v_ref[...], s0_ref[...], gam_ref[...]
    dk = q.shape[-1]
    att = jnp.sum(q.astype(BF16).astype(F32) * k.astype(BF16).astype(F32), axis=1, keepdims=True)
    qd = jnp.broadcast_to(q * gam[:, :dk], (BF16_SUBLANES, dk)).astype(BF16)
    o = att * v + jnp.dot(qd, s0.astype(BF16), preferred_element_type=F32)[0:1, :]
    s_ref[...] = s0 * gam + kcol_ref[...] * v
    o_ref[...] = _head_norm_gate(o, hg_ref[...], g_ref[...]).astype(o_ref.dtype)


def _ret_decode(q, k, v, g, s0, head_g, heads):
    db = q.shape[0]
    dk, dv = q.shape[2] // heads, v.shape[2] // heads
    assert dv >= dk
    gamma = jnp.exp(jnp.log1p(-jnp.exp2(-5.0 - jnp.arange(heads, dtype=F32))) * 1.0)
    gam = jnp.broadcast_to(gamma[:, None, None], (heads, 1, dv))
    kcol = k.reshape(db, heads, dk, 1)
    qk_spec = pl.BlockSpec((None, 1, dk), lambda b, h: (b, 0, h))
    v_spec = pl.BlockSpec((None, 1, dv), lambda b, h: (b, 0, h))
    s_spec = pl.BlockSpec((None, None, dk, dv), lambda b, h: (b, h, 0, 0))
    return pl.pallas_call(
        _ret_decode_kernel,
        grid=(db, heads),
        in_specs=[qk_spec, qk_spec,
                  pl.BlockSpec((None, None, dk, 1), lambda b, h: (b, h, 0, 0)),
                  v_spec, v_spec, s_spec,
                  pl.BlockSpec((None, 1, dv), lambda b, h: (h, 0, 0)),
                  pl.BlockSpec((1, dv), lambda b, h: (0, 0))],
        out_specs=[v_spec, s_spec],
        out_shape=[jax.ShapeDtypeStruct((db, 1, heads * dv), BF16),
                   jax.ShapeDtypeStruct(s0.shape, F32)],
        compiler_params=_params("parallel", "parallel"),
        name="ret_decode",
    )(q, k, kcol, v, g, s0, gam, head_g.reshape(1, dv))


def _rope_tables(pos, dim):
    half = dim // 2
    inv = ROPE_THETA ** (-jnp.arange(half, dtype=F32) / half)
    ang = pos.astype(F32)[:, None] * inv[None, :]
    return jnp.cos(ang), jnp.sin(ang)


def _moba_in_proj(h_p, h_s, w, qn, kn, pos_p, pos_s, width, hd):
    def tables(pos):
        cos, sin = _rope_tables(pos, hd)
        return jnp.concatenate([cos, cos], axis=-1), jnp.concatenate([-sin, sin], axis=-1)

    (cos_p, sin_p), (cos_s, sin_s) = tables(pos_p), tables(pos_s)
    rope = [("rows", cos_p, cos_s), ("rows", sin_p, sin_s)]
    qn, kn = qn.reshape(1, hd), kn.reshape(1, hd)
    q = _proj(h_p, h_s, w, 0, width, BF16, F32, _epi_headnorm_rope,
              [("const", qn, qn)] + rope, name="attn_q_proj")
    k = _proj(h_p, h_s, w, width, width, F32, F32, _epi_headnorm_rope,
              [("const", kn, kn)] + rope, name="attn_k_proj")
    v = _proj(h_p, h_s, w, 2 * width, width, F32, F32, name="attn_v_proj")
    g = _proj(h_p, h_s, w, 3 * width, width, BF16, F32, name="attn_g_proj")
    return q, k, v, g


def _ret_in_proj(h_p, h_s, w, pos_p, pos_s, qk_width, v_width, dk):
    (cos_p, sin_p), (cos_s, sin_s) = _rope_tables(pos_p, dk), _rope_tables(pos_s, dk)
    rope = [("rows", cos_p, cos_s), ("rows", sin_p, sin_s)]
    q = _proj(h_p, h_s, w, 0, qk_width, BF16, F32,
              functools.partial(_epi_rope_wide, scale=1.0), rope, name="ret_q_proj")
    k = _proj(h_p, h_s, w, qk_width, qk_width, BF16, F32,
              functools.partial(_epi_rope_wide, scale=dk ** -0.5), rope, name="ret_k_proj")
    v = _proj(h_p, h_s, w, 2 * qk_width, v_width, BF16, F32, name="ret_v_proj")
    g = _proj(h_p, h_s, w, 2 * qk_width + v_width, v_width, BF16, F32, name="ret_g_proj")
    return q, k, v, g


def _residual_proj(a_p, a_s, w, x_p, x_s, name):
    return _proj(a_p, a_s, w, 0, w.shape[1], F32, F32, _epi_residual, [("tile", x_p, x_s)],
                 name=name)


def kernel(x_prompt, x_sample, cache_k, cache_v, state_ret, page_table, attn_norm, w_in_attn,
           q_norm, k_norm, w_out_attn, ret_norm, w_in_ret, ret_head_norm, w_out_ret):
    batch, seq, d = x_prompt.shape
    db, dec_seq, _ = x_sample.shape
    assert dec_seq == 1
    n_attn, n_ret = attn_norm.shape[0], ret_norm.shape[0]
    hd = q_norm.shape[-1]
    width = w_out_attn.shape[1]
    heads = width // hd
    dv = ret_head_norm.shape[-1]
    v_width = w_out_ret.shape[1]
    ret_heads = v_width // dv
    qk_width = (w_in_ret.shape[-1] - 2 * v_width) // 2
    dk = qk_width // ret_heads
    assert dk == 2 * LANES
    n_pages = page_table.shape[1]
    past_len = n_pages * cache_k.shape[2]
    assert cache_k.shape[2] == PAGE_SIZE and past_len % MOBA_BLOCK == 0
    assert cache_k.shape[3:] == (heads, hd)

    m = batch * seq
    sp = BF16_SUBLANES * (-(-db // BF16_SUBLANES))
    xp = x_prompt.reshape(m, d)
    xs = jnp.pad(x_sample.reshape(db, d), ((0, sp - db), (0, 0)))
    pos_p = jnp.arange(seq, dtype=jnp.int32)
    pos_s = jnp.full((sp,), past_len, jnp.int32)

    def pad_rows(t):
        return jnp.pad(t.reshape(db, -1), ((0, sp - db), (0, 0)))

    kp_l, vp_l, ks_l, vs_l, sp_l, ss_l = [], [], [], [], [], []
    for layer in range(n_attn + n_ret):
        j = layer // 2
        if layer % 2 == 0:
            h_p, h_s = _rmsnorm_bf16(xp, attn_norm[j]), _rmsnorm_bf16(xs, attn_norm[j])
            (q_p, q_s), (k_p, k_s), (v_p, v_s), (g_p, g_s) = _moba_in_proj(
                h_p, h_s, w_in_attn[j], q_norm[j], k_norm[j], pos_p, pos_s, width, hd)
            o_p = _moba_prompt(q_p, k_p, v_p, g_p, batch, seq, hd)
            q_s, k_s, v_s, g_s = (t[:db].reshape(db, 1, width) for t in (q_s, k_s, v_s, g_s))
            kmean = _paged_block_means(cache_k, j, page_table)
            kmean = kmean.reshape(db, kmean.shape[1], width)
            sel = _gate_topk(q_s, kmean, heads, hd)[:, :MOBA_TOPK, :heads]
            sel = jnp.swapaxes(sel, 1, 2).reshape(-1)
            o_s = _moba_decode(q_s, k_s, v_s, g_s, cache_k, cache_v, j, sel, page_table)
            xp, xs = _residual_proj(o_p, pad_rows(o_s), w_out_attn[j], xp, xs, "attn_out_proj")
            kp_l.append(k_p.reshape(batch, seq, heads, hd))
            vp_l.append(v_p.reshape(batch, seq, heads, hd))
            ks_l.append(k_s.reshape(db, 1, heads, hd))
            vs_l.append(v_s.reshape(db, 1, heads, hd))
        else:
            h_p, h_s = _rmsnorm_bf16(xp, ret_norm[j]), _rmsnorm_bf16(xs, ret_norm[j])
            (q_p, q_s), (k_p, k_s), (v_p, v_s), (g_p, g_s) = _ret_in_proj(
                h_p, h_s, w_in_ret[j], pos_p, pos_s, qk_width, v_width, dk)
            o_p, s_p = _ret_prompt(q_p, k_p, v_p, g_p, ret_head_norm[j], batch, seq, ret_heads)
            q_s, k_s, v_s, g_s = (t[:db].reshape(db, 1, -1) for t in (q_s, k_s, v_s, g_s))
            o_s, s_s = _ret_decode(q_s, k_s, v_s, g_s, state_ret[j], ret_head_norm[j], ret_heads)
            xp, xs = _residual_proj(o_p, pad_rows(o_s), w_out_ret[j], xp, xs, "ret_out_proj")
            sp_l.append(s_p)
            ss_l.append(s_s)

    return (xp.reshape(batch, seq, d), xs[:db].reshape(db, 1, d),
            jnp.stack(kp_l), jnp.stack(vp_l), jnp.stack(ks_l), jnp.stack(vs_l),
            jnp.stack(sp_l), jnp.stack(ss_l))
```

```python
import functools
import math

import jax
import jax.numpy as jnp
from jax import lax
from jax.experimental import pallas as pl
from jax.experimental.pallas import tpu as pltpu

EPS = 1e-6
PAGE_SIZE = 128
MOBA_BLOCK = 256
MOBA_TOPK = 3
ROPE_THETA = 10000.0
RET_CHUNK = 128

F32 = jnp.float32
BF16 = jnp.bfloat16
NEG_INF = float("-inf")

LANES = 128
SUBLANES = 8
BF16_SUBLANES = 16
V7X_VMEM_BYTES = 64 * 1024 * 1024
VMEM_LIMIT_BYTES = V7X_VMEM_BYTES * 7 // 8

PROJ_TM = 1024
PROJ_TN = 512
PROJ_TK = 4096
PROJ_ROW_CHUNK = 256
PROJ_SUBTILES = 2

_NT = (((1,), (1,)), ((), ()))
_TN = (((0,), (0,)), ((), ()))


def _params(*sem):
    return pltpu.CompilerParams(dimension_semantics=sem, vmem_limit_bytes=VMEM_LIMIT_BYTES)


def _tile(n, target):
    if n <= target:
        return n
    t = target
    while n % t:
        t -= LANES if t > LANES else 1
    return t


def _silu(g):
    return g / (1.0 + jnp.exp(-g))


def _rmsnorm_kernel(x_ref, g_ref, o_ref):
    x = x_ref[...]
    ms = jnp.mean(x * x, axis=-1, keepdims=True)
    o_ref[...] = (x * lax.rsqrt(ms + EPS) * g_ref[...]).astype(o_ref.dtype)


def _rmsnorm_bf16(x, g):
    m, d = x.shape
    tm = _tile(m, 256)
    return pl.pallas_call(
        _rmsnorm_kernel,
        grid=(m // tm,),
        in_specs=[pl.BlockSpec((tm, d), lambda i: (i, 0)),
                  pl.BlockSpec((1, d), lambda i: (0, 0))],
        out_specs=pl.BlockSpec((tm, d), lambda i: (i, 0)),
        out_shape=jax.ShapeDtypeStruct((m, d), BF16),
        compiler_params=_params("parallel"),
        name="rmsnorm",
    )(x, g.reshape(1, d))


def _epi_plain(acc, rows):
    return acc


def _epi_residual(acc, rows, x_ref):
    return x_ref[rows, :] + acc


def _epi_headnorm_rope(acc, rows, gn_ref, cos_ref, sin_ref):
    hd = gn_ref.shape[-1]
    gn, cos2, sin2 = gn_ref[...], cos_ref[rows, :], sin_ref[rows, :]
    outs = []
    for c in range(acc.shape[1] // hd):
        z = acc[:, c * hd:(c + 1) * hd]
        ms = jnp.mean(z * z, axis=-1, keepdims=True)
        zn = z * lax.rsqrt(ms + EPS) * gn
        outs.append(zn * cos2 + pltpu.roll(zn, hd // 2, 1) * sin2)
    return jnp.concatenate(outs, axis=1)


def _epi_rope_wide(acc, rows, cos_ref, sin_ref, *, scale):
    half = cos_ref.shape[-1]
    cos, sin = cos_ref[rows, :], sin_ref[rows, :]
    outs = []
    for c in range(acc.shape[1] // (2 * half)):
        x1 = acc[:, (2 * c) * half:(2 * c + 1) * half]
        x2 = acc[:, (2 * c + 1) * half:(2 * c + 2) * half]
        outs.append((x1 * cos - x2 * sin) * scale)
        outs.append((x2 * cos + x1 * sin) * scale)
    return jnp.concatenate(outs, axis=1)


def _proj_kernel(*refs, epilogue, n_extra, nk, row_chunk):
    ap_ref, as_ref, w_ref = refs[:3]
    extras_p = refs[3:3 + n_extra]
    extras_s = refs[3 + n_extra:3 + 2 * n_extra]
    op_ref, os_ref, wbf_ref = refs[3 + 2 * n_extra:6 + 2 * n_extra]
    acc_refs = refs[6 + 2 * n_extra:]
    i, t = pl.program_id(1), pl.program_id(2)
    tm = ap_ref.shape[0]
    all_rows = slice(None)

    def tile(a_ref, acc_ref, o_ref, extras, row_slices):
        for rows in row_slices:
            acc = jnp.dot(a_ref[rows, :], wbf_ref[t], preferred_element_type=F32)
            if nk == 1:
                o_ref[rows, :] = epilogue(acc, rows, *extras).astype(o_ref.dtype)
                continue

            @pl.when(t == 0)
            def _():
                acc_ref[rows, :] = acc

            if nk > 2:
                @pl.when(jnp.logical_and(t > 0, t < nk - 1))
                def _():
                    acc_ref[rows, :] += acc

            @pl.when(t == nk - 1)
            def _():
                o_ref[rows, :] = epilogue(acc_ref[rows, :] + acc, rows, *extras).astype(o_ref.dtype)

    @pl.when(i == 0)
    def _():
        wbf_ref[t] = w_ref[...].astype(BF16)
        tile(as_ref, acc_refs[1] if nk > 1 else None, os_ref, extras_s, [all_rows])

    tile(ap_ref, acc_refs[0] if nk > 1 else None, op_ref, extras_p,
         [slice(r, r + row_chunk) for r in range(0, tm, row_chunk)])


def _proj(a_p, a_s, w, col0, n_out, dtype_p, dtype_s, epilogue=_epi_plain, extras=(), name="proj"):
    m, k = a_p.shape
    s = a_s.shape[0]
    tm, tn, tk = _tile(m, PROJ_TM), _tile(n_out, PROJ_TN), _tile(k, PROJ_TK)
    nk = k // tk
    n_tiles = n_out // tn
    n_sub = PROJ_SUBTILES if nk == 1 and n_tiles % PROJ_SUBTILES == 0 else 1
    steps = n_sub if nk == 1 else nk
    row_chunk = _tile(tm, PROJ_ROW_CHUNK)
    assert col0 % tn == 0
    off = col0 // tn

    def col(g, i, t):
        return g * n_sub + t if nk == 1 else g

    def col_first_rows_only(g, i, t):
        return jnp.where(i == 0, col(g, i, t), g * n_sub + n_sub - 1) if nk == 1 else g

    def kchunk(t):
        return 0 if nk == 1 else t

    def extra_specs(kind, arr, rows_per_tile, is_prompt):
        if kind == "const":
            return pl.BlockSpec(arr.shape, lambda g, i, t: (0,) * arr.ndim)
        if kind == "rows":
            assert arr.shape[0] % rows_per_tile == 0
            period = arr.shape[0] // rows_per_tile
            if is_prompt:
                return pl.BlockSpec((rows_per_tile, arr.shape[1]), lambda g, i, t: (i % period, 0))
            return pl.BlockSpec((rows_per_tile, arr.shape[1]), lambda g, i, t: (0, 0))
        assert kind == "tile"
        if is_prompt:
            return pl.BlockSpec((rows_per_tile, tn), lambda g, i, t: (i, col(g, i, t)))
        return pl.BlockSpec((rows_per_tile, tn), lambda g, i, t: (0, col_first_rows_only(g, i, t)))

    in_specs = [pl.BlockSpec((tm, tk), lambda g, i, t: (i, kchunk(t))),
                pl.BlockSpec((s, tk), lambda g, i, t: (0, kchunk(t))),
                pl.BlockSpec((tk, tn),
                             lambda g, i, t: (kchunk(t), off + col_first_rows_only(g, i, t)))]
    in_specs += [extra_specs(kind, ap, tm, True) for kind, ap, _ in extras]
    in_specs += [extra_specs(kind, asm, s, False) for kind, _, asm in extras]
    operands = [a_p, a_s, w] + [ap for _, ap, _ in extras] + [asm for _, _, asm in extras]
    scratch = [pltpu.VMEM((steps, tk, tn), BF16)]
    if nk > 1:
        scratch += [pltpu.VMEM((tm, tn), F32), pltpu.VMEM((s, tn), F32)]
    return pl.pallas_call(
        functools.partial(_proj_kernel, epilogue=epilogue, n_extra=len(extras), nk=nk,
                          row_chunk=row_chunk),
        grid=(n_tiles // n_sub, m // tm, steps),
        in_specs=in_specs,
        out_specs=[pl.BlockSpec((tm, tn), lambda g, i, t: (i, col(g, i, t))),
                   pl.BlockSpec((s, tn), lambda g, i, t: (0, col_first_rows_only(g, i, t)))],
        out_shape=[jax.ShapeDtypeStruct((m, n_out), dtype_p),
                   jax.ShapeDtypeStruct((s, n_out), dtype_s)],
        scratch_shapes=scratch,
        compiler_params=_params("parallel", "arbitrary", "arbitrary"),
        name=name,
    )(*operands)


def _moba_prompt_kernel(q_ref, k_ref, v_ref, g_ref, o_ref, k16_ref, vt16_ref, *, scale):
    seq, hd = q_ref.shape
    blk, topk = MOBA_BLOCK, MOBA_TOPK
    nb = seq // blk
    k16_ref[...] = k_ref[...].astype(BF16)
    vt16_ref[...] = v_ref[...].T.astype(BF16)
    km_rows = SUBLANES * max(1, -(-(nb - 1) // SUBLANES))
    row_id = lax.broadcasted_iota(jnp.int32, (km_rows, hd), 0)
    kmean = jnp.zeros((km_rows, hd), F32)
    for n in range(nb - 1):
        km = jnp.mean(k_ref[n * blk:(n + 1) * blk, :], axis=0, keepdims=True)
        kmean = jnp.where(row_id == n, km, kmean)
    kmean16 = kmean.astype(BF16)
    key_id = lax.broadcasted_iota(jnp.int32, (blk, blk), 0)
    qry_id = lax.broadcasted_iota(jnp.int32, (blk, blk), 1)
    causal_bias = jnp.where(key_id <= qry_id, 0.0, NEG_INF).astype(F32)
    c = scale * math.log2(math.e)

    def scores(j):
        qj = q_ref[j * blk:(j + 1) * blk, :]
        st = lax.dot_general(k16_ref[0:(j + 1) * blk, :], qj, _NT,
                             preferred_element_type=F32)
        gt = None
        if j > topk:
            gt = lax.dot_general(kmean16, qj, _NT, preferred_element_type=F32)
        return st, gt

    def softmax(j, st, gt):
        biases = None
        if j > topk:
            gates = [gt[n:n + 1, :] for n in range(j)]
            ranks = [jnp.zeros((1, blk), F32) for _ in range(j)]
            for lo in range(j):
                for hi in range(lo + 1, j):
                    lo_ahead = jnp.where(gates[lo] >= gates[hi], 1.0, 0.0)
                    ranks[hi] = ranks[hi] + lo_ahead
                    ranks[lo] = ranks[lo] + (1.0 - lo_ahead)
            biases = [jnp.where(r < topk, 0.0, NEG_INF).astype(F32) for r in ranks]
        pieces = []
        for n in range(j + 1):
            sn = st[n * blk:(n + 1) * blk, :]
            if n == j:
                sn = sn + causal_bias
            elif biases is not None:
                sn = sn + biases[n]
            pieces.append(sn)
        mx = jnp.max(pieces[0], axis=0, keepdims=True)
        for sn in pieces[1:]:
            mx = jnp.maximum(mx, jnp.max(sn, axis=0, keepdims=True))
        probs = [jnp.exp2((sn - mx) * c) for sn in pieces]
        denom = jnp.sum(probs[0], axis=0, keepdims=True)
        for p in probs[1:]:
            denom = denom + jnp.sum(p, axis=0, keepdims=True)
        return jnp.concatenate(probs, axis=0).astype(BF16), denom

    def finish(j, pt16, denom):
        rows = slice(j * blk, (j + 1) * blk)
        ot = jnp.dot(vt16_ref[:, 0:(j + 1) * blk], pt16, preferred_element_type=F32) / denom
        o_ref[rows, :] = (ot.T * _silu(g_ref[rows, :].astype(F32))).astype(o_ref.dtype)

    nxt = scores(0)
    pending = None
    for j in range(nb):
        cur = nxt
        if j + 1 < nb:
            nxt = scores(j + 1)
        if pending is not None:
            finish(*pending)
        pending = (j,) + softmax(j, *cur)
    finish(*pending)


def _moba_prompt(q, k, v, g, batch, seq, hd):
    m, width = q.shape
    heads = width // hd
    assert seq % MOBA_BLOCK == 0 and hd == LANES
    spec = pl.BlockSpec((seq, hd), lambda b, h: (b, h))
    return pl.pallas_call(
        functools.partial(_moba_prompt_kernel, scale=hd ** -0.5),
        grid=(batch, heads),
        in_specs=[spec, spec, spec, spec],
        out_specs=spec,
        out_shape=jax.ShapeDtypeStruct((m, width), BF16),
        scratch_shapes=[pltpu.VMEM((seq, hd), BF16), pltpu.VMEM((hd, seq), BF16)],
        compiler_params=_params("parallel", "parallel"),
        name="moba_prompt",
    )(q, k, v, g)


def _kmean_kernel(pt_ref, *refs, inv_count):
    *page_refs, o_ref = refs
    tot = jnp.sum(page_refs[0][...], axis=0)
    for r in page_refs[1:]:
        tot = tot + jnp.sum(r[...], axis=0)
    o_ref[...] = tot * inv_count


def _paged_block_means(cache, layer, page_table):
    _, _, page, heads, hd = cache.shape
    db, n_pages = page_table.shape
    ppb = MOBA_BLOCK // page
    assert n_pages % ppb == 0
    nblk = n_pages // ppb

    def page_spec(r):
        return pl.BlockSpec((None, None, page, heads, hd),
                            lambda b, n, pt: (layer, pt[b * n_pages + n * ppb + r], 0, 0, 0))

    return pl.pallas_call(
        functools.partial(_kmean_kernel, inv_count=1.0 / MOBA_BLOCK),
        grid_spec=pltpu.PrefetchScalarGridSpec(
            num_scalar_prefetch=1,
            grid=(db, nblk),
            in_specs=[page_spec(r) for r in range(ppb)],
            out_specs=pl.BlockSpec((None, None, heads, hd), lambda b, n, pt: (b, n, 0, 0)),
        ),
        out_shape=jax.ShapeDtypeStruct((db, nblk, heads, hd), F32),
        compiler_params=_params("parallel", "arbitrary"),
        name="paged_block_means",
    )(page_table.reshape(-1), *([cache] * ppb))


def _gate_topk_kernel(q_ref, km_ref, o_ref, *, heads, hd):
    q = q_ref[...].astype(BF16).astype(F32)
    km = km_ref[...].astype(BF16).astype(F32)
    prod = km * q
    nblk = km.shape[0]
    lane = lax.broadcasted_iota(jnp.int32, (nblk, LANES), 1)
    gate = jnp.full((nblk, LANES), NEG_INF, F32)
    for h in range(heads):
        gh = jnp.sum(prod[:, h * hd:(h + 1) * hd], axis=1, keepdims=True)
        gate = jnp.where(lane == h, gh, gate)
    rowf = lax.broadcasted_iota(jnp.int32, (nblk, LANES), 0).astype(F32)
    out_row = lax.broadcasted_iota(jnp.int32, o_ref.shape, 0)
    out = jnp.zeros(o_ref.shape, F32)
    for t in range(MOBA_TOPK):
        mx = jnp.max(gate, axis=0, keepdims=True)
        idx = jnp.min(jnp.where(gate == mx, rowf, float(nblk)), axis=0, keepdims=True)
        out = jnp.where(out_row == t, idx, out)
        gate = jnp.where(rowf == idx, NEG_INF, gate)
    o_ref[...] = out.astype(jnp.int32)


def _gate_topk(q, kmean, heads, hd):
    db, nblk, width = kmean.shape
    assert heads <= LANES and nblk >= MOBA_TOPK
    return pl.pallas_call(
        functools.partial(_gate_topk_kernel, heads=heads, hd=hd),
        grid=(db,),
        in_specs=[pl.BlockSpec((None, 1, width), lambda b: (b, 0, 0)),
                  pl.BlockSpec((None, nblk, width), lambda b: (b, 0, 0))],
        out_specs=pl.BlockSpec((None, SUBLANES, LANES), lambda b: (b, 0, 0)),
        out_shape=jax.ShapeDtypeStruct((db, SUBLANES, LANES), jnp.int32),
        compiler_params=_params("parallel"),
        name="gate_topk",
    )(q, kmean)


def _moba_decode_kernel(sel_ref, pt_ref, q_ref, kn_ref, vn_ref, g_ref, *refs, scale):
    *page_refs, o_ref = refs
    n = len(page_refs) // 2
    k_pages, v_pages = page_refs[:n], page_refs[n:]
    page, group, hd = k_pages[0].shape
    head_in_group = pl.program_id(1) % group
    row_head = lax.broadcasted_iota(jnp.int32, (1, page * group), 1) % group
    head_bias = jnp.where(row_head == head_in_group, 0.0, NEG_INF).astype(F32)

    def flat16(ref):
        return ref[...].reshape(page * group, hd).astype(BF16)

    q = q_ref[...]
    q16 = jnp.broadcast_to(q, (BF16_SUBLANES, hd)).astype(BF16)
    scores = [lax.dot_general(q16, flat16(kp), _NT, preferred_element_type=F32)[0:1, :] * scale
              + head_bias for kp in k_pages]
    qb = q.astype(BF16).astype(F32)
    s_new = jnp.sum(qb * kn_ref[...].astype(BF16).astype(F32), axis=1, keepdims=True) * scale
    mx = s_new
    for s in scores:
        mx = jnp.maximum(mx, jnp.max(s, axis=1, keepdims=True))
    p_new = jnp.exp(s_new - mx)
    denom = p_new
    acc = p_new.astype(BF16).astype(F32) * vn_ref[...].astype(BF16).astype(F32)
    for s, vp in zip(scores, v_pages):
        p = jnp.exp(s - mx)
        denom = denom + jnp.sum(p, axis=1, keepdims=True)
        p16 = jnp.broadcast_to(p, (BF16_SUBLANES, p.shape[1])).astype(BF16)
        acc = acc + jnp.dot(p16, flat16(vp), preferred_element_type=F32)[0:1, :]
    o = acc / denom
    o_ref[...] = (o * _silu(g_ref[...])).astype(o_ref.dtype)


def _moba_decode(q, k_new, v_new, g, cache_k, cache_v, layer, sel, page_table):
    db, _, width = q.shape
    _, _, page, heads, hd = cache_k.shape
    n_pages = page_table.shape[1]
    ppb = MOBA_BLOCK // page
    group = min(heads, SUBLANES)
    assert heads % group == 0

    row_spec = pl.BlockSpec((None, 1, hd), lambda b, h, sel, pt: (b, 0, h))

    def page_spec(t, r):
        def imap(b, h, sel, pt):
            blk = sel[(b * heads + h) * MOBA_TOPK + t]
            return (layer, pt[b * n_pages + blk * ppb + r], 0, h // group, 0)
        return pl.BlockSpec((None, None, page, group, hd), imap)

    page_specs = [page_spec(t, r) for t in range(MOBA_TOPK) for r in range(ppb)]
    n = len(page_specs)
    return pl.pallas_call(
        functools.partial(_moba_decode_kernel, scale=hd ** -0.5),
        grid_spec=pltpu.PrefetchScalarGridSpec(
            num_scalar_prefetch=2,
            grid=(db, heads),
            in_specs=[row_spec] * 4 + page_specs + page_specs,
            out_specs=row_spec,
        ),
        out_shape=jax.ShapeDtypeStruct((db, 1, width), BF16),
        compiler_params=_params("parallel", "parallel"),
        name="moba_decode",
    )(sel, page_table.reshape(-1), q, k_new, v_new, g, *([cache_k] * n), *([cache_v] * n))


def _head_norm_gate(o, hg, g):
    ms = jnp.mean(o * o, axis=-1, keepdims=True)
    return (o * lax.rsqrt(ms + EPS) * hg) * _silu(g)


def _ret_prompt_kernel(q_ref, k_ref, v_ref, g_ref, dm_ref, qd_ref, kd_ref, cd_ref, hg_ref,
                       o_ref, s_ref):
    c = dm_ref.shape[-1]
    n_chunks = q_ref.shape[0] // c
    s_ref[...] = jnp.zeros(s_ref.shape, F32)
    dm, qdec, kdec, cdec, hg = dm_ref[...], qd_ref[...], kd_ref[...], cd_ref[...], hg_ref[...]

    def local(ci):
        r = slice(ci * c, (ci + 1) * c)
        qc, kc, vc = q_ref[r, :], k_ref[r, :], v_ref[r, :]
        att = lax.dot_general(qc, kc, _NT, preferred_element_type=F32) * dm
        inner = jnp.dot(att.astype(BF16), vc, preferred_element_type=F32)
        qd = (qc.astype(F32) * qdec).astype(BF16)
        kd = (kc.astype(F32) * kdec).astype(BF16)
        return inner, qd, lax.dot_general(kd, vc, _TN, preferred_element_type=F32)

    def carry(ci, inner, qd, kv):
        r = slice(ci * c, (ci + 1) * c)
        s = s_ref[...]
        o = inner + jnp.dot(qd, s.astype(BF16), preferred_element_type=F32)
        s_ref[...] = s * cdec + kv
        o_ref[r, :] = _head_norm_gate(o, hg, g_ref[r, :].astype(F32)).astype(o_ref.dtype)

    nxt = local(0)
    for ci in range(n_chunks):
        cur = nxt
        if ci + 1 < n_chunks:
            nxt = local(ci + 1)
        carry(ci, *cur)


def _ret_tables(heads, c, dk, dv):
    log_g = jnp.log1p(-jnp.exp2(-5.0 - jnp.arange(heads, dtype=F32)))
    i = jnp.arange(c, dtype=F32)
    diff = i[:, None] - i[None, :]
    dmask = jnp.where(diff >= 0, jnp.exp(log_g[:, None, None] * jnp.maximum(diff, 0.0)), 0.0)
    q_dec = jnp.exp(log_g[:, None] * (i[None, :] + 1.0))
    k_dec = jnp.exp(log_g[:, None] * (c - 1.0 - i[None, :]))
    chunk_dec = jnp.exp(log_g * c)
    return (dmask,
            jnp.broadcast_to(q_dec[:, :, None], (heads, c, dk)),
            jnp.broadcast_to(k_dec[:, :, None], (heads, c, dk)),
            jnp.broadcast_to(chunk_dec[:, None, None], (heads, 1, dv)))


def _ret_prompt(q, k, v, g, head_g, batch, seq, heads):
    m = q.shape[0]
    dk, dv = q.shape[1] // heads, v.shape[1] // heads
    c = RET_CHUNK
    assert seq % c == 0
    dmask, qdec, kdec, cdec = _ret_tables(heads, c, dk, dv)
    qk_spec = pl.BlockSpec((seq, dk), lambda b, h: (b, h))
    v_spec = pl.BlockSpec((seq, dv), lambda b, h: (b, h))

    def head_spec(*shape):
        return pl.BlockSpec((None,) + shape, lambda b, h: (h, 0, 0))

    return pl.pallas_call(
        _ret_prompt_kernel,
        grid=(batch, heads),
        in_specs=[qk_spec, qk_spec, v_spec, v_spec, head_spec(c, c), head_spec(c, dk),
                  head_spec(c, dk), head_spec(1, dv), pl.BlockSpec((1, dv), lambda b, h: (0, 0))],
        out_specs=[v_spec, pl.BlockSpec((None, None, dk, dv), lambda b, h: (b, h, 0, 0))],
        out_shape=[jax.ShapeDtypeStruct((m, heads * dv), BF16),
                   jax.ShapeDtypeStruct((batch, heads, dk, dv), F32)],
        compiler_params=_params("parallel", "parallel"),
        name="ret_prompt",
    )(q, k, v, g, dmask, qdec, kdec, cdec, head_g.reshape(1, dv))


def _ret_decode_kernel(q_ref, k_ref, kcol_ref, v_ref, g_ref, s0_ref, gam_ref, hg_ref,
                       o_ref, s_ref):
    q, k, v, s0, gam = q_ref[...], k_ref[...], v_ref[...], s0_ref[...], gam_ref[...]
    dk = q.shape[-1]
    att = jnp.sum(q.astype(BF16).astype(F32) * k.astype(BF16).astype(F32), axis=1, keepdims=True)
    qd = jnp.broadcast_to(q * gam[:, :dk], (BF16_SUBLANES, dk)).astype(BF16)
    o = att * v + jnp.dot(qd, s0.astype(BF16), preferred_element_type=F32)[0:1, :]
    s_ref[...] = s0 * gam + kcol_ref[...] * v
    o_ref[...] = _head_norm_gate(o, hg_ref[...], g_ref[...]).astype(o_ref.dtype)


def _ret_decode(q, k, v, g, s0, head_g, heads):
    db = q.shape[0]
    dk, dv = q.shape[2] // heads, v.shape[2] // heads
    assert dv >= dk
    gamma = jnp.exp(jnp.log1p(-jnp.exp2(-5.0 - jnp.arange(heads, dtype=F32))) * 1.0)
    gam = jnp.broadcast_to(gamma[:, None, None], (heads, 1, dv))
    kcol = k.reshape(db, heads, dk, 1)
    qk_spec = pl.BlockSpec((None, 1, dk), lambda b, h: (b, 0, h))
    v_spec = pl.BlockSpec((None, 1, dv), lambda b, h: (b, 0, h))
    s_spec = pl.BlockSpec((None, None, dk, dv), lambda b, h: (b, h, 0, 0))
    return pl.pallas_call(
        _ret_decode_kernel,
        grid=(db, heads),
        in_specs=[qk_spec, qk_spec,
                  pl.BlockSpec((None, None, dk, 1), lambda b, h: (b, h, 0, 0)),
                  v_spec, v_spec, s_spec,
                  pl.BlockSpec((None, 1, dv), lambda b, h: (h, 0, 0)),
                  pl.BlockSpec((1, dv), lambda b, h: (0, 0))],
        out_specs=[v_spec, s_spec],
        out_shape=[jax.ShapeDtypeStruct((db, 1, heads * dv), BF16),
                   jax.ShapeDtypeStruct(s0.shape, F32)],
        compiler_params=_params("parallel", "parallel"),
        name="ret_decode",
    )(q, k, kcol, v, g, s0, gam, head_g.reshape(1, dv))


def _rope_tables(pos, dim):
    half = dim // 2
    inv = ROPE_THETA ** (-jnp.arange(half, dtype=F32) / half)
    ang = pos.astype(F32)[:, None] * inv[None, :]
    return jnp.cos(ang), jnp.sin(ang)


def _moba_in_proj(h_p, h_s, w, qn, kn, pos_p, pos_s, width, hd):
    def tables(pos):
        cos, sin = _rope_tables(pos, hd)
        return jnp.concatenate([cos, cos], axis=-1), jnp.concatenate([-sin, sin], axis=-1)

    (cos_p, sin_p), (cos_s, sin_s) = tables(pos_p), tables(pos_s)
    rope = [("rows", cos_p, cos_s), ("rows", sin_p, sin_s)]
    qn, kn = qn.reshape(1, hd), kn.reshape(1, hd)
    q = _proj(h_p, h_s, w, 0, width, BF16, F32, _epi_headnorm_rope,
              [("const", qn, qn)] + rope, name="attn_q_proj")
    k = _proj(h_p, h_s, w, width, width, F32, F32, _epi_headnorm_rope,
              [("const", kn, kn)] + rope, name="attn_k_proj")
    v = _proj(h_p, h_s, w, 2 * width, width, F32, F32, name="attn_v_proj")
    g = _proj(h_p, h_s, w, 3 * width, width, BF16, F32, name="attn_g_proj")
    return q, k, v, g


def _ret_in_proj(h_p, h_s, w, pos_p, pos_s, qk_width, v_width, dk):
    (cos_p, sin_p), (cos_s, sin_s) = _rope_tables(pos_p, dk), _rope_tables(pos_s, dk)
    rope = [("rows", cos_p, cos_s), ("rows", sin_p, sin_s)]
    q = _proj(h_p, h_s, w, 0, qk_width, BF16, F32,
              functools.partial(_epi_rope_wide, scale=1.0), rope, name="ret_q_proj")
    k = _proj(h_p, h_s, w, qk_width, qk_width, BF16, F32,
              functools.partial(_epi_rope_wide, scale=dk ** -0.5), rope, name="ret_k_proj")
    v = _proj(h_p, h_s, w, 2 * qk_width, v_width, BF16, F32, name="ret_v_proj")
    g = _proj(h_p, h_s, w, 2 * qk_width + v_width, v_width, BF16, F32, name="ret_g_proj")
    return q, k, v, g


def _residual_proj(a_p, a_s, w, x_p, x_s, name):
    return _proj(a_p, a_s, w, 0, w.shape[1], F32, F32, _epi_residual, [("tile", x_p, x_s)],
                 name=name)


def kernel(x_prompt, x_sample, cache_k, cache_v, state_ret, page_table, attn_norm, w_in_attn,
           q_norm, k_norm, w_out_attn, ret_norm, w_in_ret, ret_head_norm, w_out_ret):
    batch, seq, d = x_prompt.shape
    db, dec_seq, _ = x_sample.shape
    assert dec_seq == 1
    n_attn, n_ret = attn_norm.shape[0], ret_norm.shape[0]
    hd = q_norm.shape[-1]
    width = w_out_attn.shape[1]
    heads = width // hd
    dv = ret_head_norm.shape[-1]
    v_width = w_out_ret.shape[1]
    ret_heads = v_width // dv
    qk_width = (w_in_ret.shape[-1] - 2 * v_width) // 2
    dk = qk_width // ret_heads
    assert dk == 2 * LANES
    n_pages = page_table.shape[1]
    past_len = n_pages * cache_k.shape[2]
    assert cache_k.shape[2] == PAGE_SIZE and past_len % MOBA_BLOCK == 0
    assert cache_k.shape[3:] == (heads, hd)

    m = batch * seq
    sp = BF16_SUBLANES * (-(-db // BF16_SUBLANES))
    xp = x_prompt.reshape(m, d)
    xs = jnp.pad(x_sample.reshape(db, d), ((0, sp - db), (0, 0)))
    pos_p = jnp.arange(seq, dtype=jnp.int32)
    pos_s = jnp.full((sp,), past_len, jnp.int32)

    def pad_rows(t):
        return jnp.pad(t.reshape(db, -1), ((0, sp - db), (0, 0)))

    kp_l, vp_l, ks_l, vs_l, sp_l, ss_l = [], [], [], [], [], []
    for layer in range(n_attn + n_ret):
        j = layer // 2
        if layer % 2 == 0:
            h_p, h_s = _rmsnorm_bf16(xp, attn_norm[j]), _rmsnorm_bf16(xs, attn_norm[j])
            (q_p, q_s), (k_p, k_s), (v_p, v_s), (g_p, g_s) = _moba_in_proj(
                h_p, h_s, w_in_attn[j], q_norm[j], k_norm[j], pos_p, pos_s, width, hd)
            o_p = _moba_prompt(q_p, k_p, v_p, g_p, batch, seq, hd)
            q_s, k_s, v_s, g_s = (t[:db].reshape(db, 1, width) for t in (q_s, k_s, v_s, g_s))
            kmean = _paged_block_means(cache_k, j, page_table)
            kmean = kmean.reshape(db, kmean.shape[1], width)
            sel = _gate_topk(q_s, kmean, heads, hd)[:, :MOBA_TOPK, :heads]
            sel = jnp.swapaxes(sel, 1, 2).reshape(-1)
            o_s = _moba_decode(q_s, k_s, v_s, g_s, cache_k, cache_v, j, sel, page_table)
            xp, xs = _residual_proj(o_p, pad_rows(o_s), w_out_attn[j], xp, xs, "attn_out_proj")
            kp_l.append(k_p.reshape(batch, seq, heads, hd))
            vp_l.append(v_p.reshape(batch, seq, heads, hd))
            ks_l.append(k_s.reshape(db, 1, heads, hd))
            vs_l.append(v_s.reshape(db, 1, heads, hd))
        else:
            h_p, h_s = _rmsnorm_bf16(xp, ret_norm[j]), _rmsnorm_bf16(xs, ret_norm[j])
            (q_p, q_s), (k_p, k_s), (v_p, v_s), (g_p, g_s) = _ret_in_proj(
                h_p, h_s, w_in_ret[j], pos_p, pos_s, qk_width, v_width, dk)
            o_p, s_p = _ret_prompt(q_p, k_p, v_p, g_p, ret_head_norm[j], batch, seq, ret_heads)
            q_s, k_s, v_s, g_s = (t[:db].reshape(db, 1, -1) for t in (q_s, k_s, v_s, g_s))
            o_s, s_s = _ret_decode(q_s, k_s, v_s, g_s, state_ret[j], ret_head_norm[j], ret_heads)
            xp, xs = _residual_proj(o_p, pad_rows(o_s), w_out_ret[j], xp, xs, "ret_out_proj")
            sp_l.append(s_p)
            ss_l.append(s_s)

    return (xp.reshape(batch, seq, d), xs[:db].reshape(db, 1, d),
            jnp.stack(kp_l), jnp.stack(vp_l), jnp.stack(ks_l), jnp.stack(vs_l),
            jnp.stack(sp_l), jnp.stack(ss_l))
```

```python
import functools
import math

import jax
import jax.numpy as jnp
from jax import lax
from jax.experimental import pallas as pl
from jax.experimental.pallas import tpu as pltpu

EPS = 1e-6
PAGE_SIZE = 128
MOBA_BLOCK = 256
MOBA_TOPK = 3
ROPE_THETA = 10000.0
RET_CHUNK = 128

F32 = jnp.float32
BF16 = jnp.bfloat16
NEG_INF = float("-inf")

LANES = 128
SUBLANES = 8
BF16_SUBLANES = 16
V7X_VMEM_BYTES = 64 * 1024 * 1024
VMEM_LIMIT_BYTES = V7X_VMEM_BYTES * 7 // 8

PROJ_TM = 1024
PROJ_TN = 512
PROJ_TK = 4096
PROJ_ROW_CHUNK = 256
PROJ_SUBTILES = 2
KMEAN_BLOCKS_PER_STEP = 2

_NT = (((1,), (1,)), ((), ()))
_TN = (((0,), (0,)), ((), ()))


def _params(*sem):
    return pltpu.CompilerParams(dimension_semantics=sem, vmem_limit_bytes=VMEM_LIMIT_BYTES)


def _tile(n, target):
    if n <= target:
        return n
    t = target
    while n % t:
        t -= LANES if t > LANES else 1
    return t


def _silu(g):
    return g / (1.0 + jnp.exp(-g))


def _rmsnorm_kernel(x_ref, g_ref, o_ref):
    x = x_ref[...]
    ms = jnp.mean(x * x, axis=-1, keepdims=True)
    o_ref[...] = (x * lax.rsqrt(ms + EPS) * g_ref[...]).astype(o_ref.dtype)


def _rmsnorm_bf16(x, g):
    m, d = x.shape
    tm = _tile(m, 256)
    return pl.pallas_call(
        _rmsnorm_kernel,
        grid=(m // tm,),
        in_specs=[pl.BlockSpec((tm, d), lambda i: (i, 0)),
                  pl.BlockSpec((1, d), lambda i: (0, 0))],
        out_specs=pl.BlockSpec((tm, d), lambda i: (i, 0)),
        out_shape=jax.ShapeDtypeStruct((m, d), BF16),
        compiler_params=_params("parallel"),
        name="rmsnorm",
    )(x, g.reshape(1, d))


def _epi_plain(acc, rows):
    return acc


def _epi_residual(acc, rows, x_ref):
    return x_ref[rows, :] + acc


def _epi_headnorm_rope(acc, rows, gn_ref, cos_ref, sin_ref):
    hd = gn_ref.shape[-1]
    gn, cos2, sin2 = gn_ref[...], cos_ref[rows, :], sin_ref[rows, :]
    outs = []
    for c in range(acc.shape[1] // hd):
        z = acc[:, c * hd:(c + 1) * hd]
        ms = jnp.mean(z * z, axis=-1, keepdims=True)
        zn = z * lax.rsqrt(ms + EPS) * gn
        outs.append(zn * cos2 + pltpu.roll(zn, hd // 2, 1) * sin2)
    return jnp.concatenate(outs, axis=1)


def _epi_rope_wide(acc, rows, cos_ref, sin_ref, *, scale):
    half = cos_ref.shape[-1]
    cos, sin = cos_ref[rows, :], sin_ref[rows, :]
    outs = []
    for c in range(acc.shape[1] // (2 * half)):
        x1 = acc[:, (2 * c) * half:(2 * c + 1) * half]
        x2 = acc[:, (2 * c + 1) * half:(2 * c + 2) * half]
        outs.append((x1 * cos - x2 * sin) * scale)
        outs.append((x2 * cos + x1 * sin) * scale)
    return jnp.concatenate(outs, axis=1)


def _proj_kernel(*refs, epilogue, n_extra, nk, row_chunk):
    ap_ref, as_ref, w_ref = refs[:3]
    extras_p = refs[3:3 + n_extra]
    extras_s = refs[3 + n_extra:3 + 2 * n_extra]
    op_ref, os_ref, wbf_ref = refs[3 + 2 * n_extra:6 + 2 * n_extra]
    acc_refs = refs[6 + 2 * n_extra:]
    i, t = pl.program_id(1), pl.program_id(2)
    tm = ap_ref.shape[0]
    all_rows = slice(None)

    def tile(a_ref, acc_ref, o_ref, extras, row_slices):
        for rows in row_slices:
            acc = jnp.dot(a_ref[rows, :], wbf_ref[t], preferred_element_type=F32)
            if nk == 1:
                o_ref[rows, :] = epilogue(acc, rows, *extras).astype(o_ref.dtype)
                continue

            @pl.when(t == 0)
            def _():
                acc_ref[rows, :] = acc

            if nk > 2:
                @pl.when(jnp.logical_and(t > 0, t < nk - 1))
                def _():
                    acc_ref[rows, :] += acc

            @pl.when(t == nk - 1)
            def _():
                o_ref[rows, :] = epilogue(acc_ref[rows, :] + acc, rows, *extras).astype(o_ref.dtype)

    @pl.when(i == 0)
    def _():
        wbf_ref[t] = w_ref[...].astype(BF16)
        tile(as_ref, acc_refs[1] if nk > 1 else None, os_ref, extras_s, [all_rows])

    tile(ap_ref, acc_refs[0] if nk > 1 else None, op_ref, extras_p,
         [slice(r, r + row_chunk) for r in range(0, tm, row_chunk)])


def _proj(a_p, a_s, w, col0, n_out, dtype_p, dtype_s, epilogue=_epi_plain, extras=(), name="proj",
          row_chunk=PROJ_ROW_CHUNK):
    m, k = a_p.shape
    s = a_s.shape[0]
    tm, tn, tk = _tile(m, PROJ_TM), _tile(n_out, PROJ_TN), _tile(k, PROJ_TK)
    nk = k // tk
    n_tiles = n_out // tn
    n_sub = PROJ_SUBTILES if nk == 1 and n_tiles % PROJ_SUBTILES == 0 else 1
    steps = n_sub if nk == 1 else nk
    row_chunk = _tile(tm, row_chunk)
    assert col0 % tn == 0
    off = col0 // tn

    def col(g, i, t):
        return g * n_sub + t if nk == 1 else g

    def col_first_rows_only(g, i, t):
        return jnp.where(i == 0, col(g, i, t), g * n_sub + n_sub - 1) if nk == 1 else g

    def kchunk(t):
        return 0 if nk == 1 else t

    def extra_specs(kind, arr, rows_per_tile, is_prompt):
        if kind == "const":
            return pl.BlockSpec(arr.shape, lambda g, i, t: (0,) * arr.ndim)
        if kind == "rows":
            assert arr.shape[0] % rows_per_tile == 0
            period = arr.shape[0] // rows_per_tile
            if is_prompt:
                return pl.BlockSpec((rows_per_tile, arr.shape[1]), lambda g, i, t: (i % period, 0))
            return pl.BlockSpec((rows_per_tile, arr.shape[1]), lambda g, i, t: (0, 0))
        assert kind == "tile"
        if is_prompt:
            return pl.BlockSpec((rows_per_tile, tn), lambda g, i, t: (i, col(g, i, t)))
        return pl.BlockSpec((rows_per_tile, tn), lambda g, i, t: (0, col_first_rows_only(g, i, t)))

    in_specs = [pl.BlockSpec((tm, tk), lambda g, i, t: (i, kchunk(t))),
                pl.BlockSpec((s, tk), lambda g, i, t: (0, kchunk(t))),
                pl.BlockSpec((tk, tn),
                             lambda g, i, t: (kchunk(t), off + col_first_rows_only(g, i, t)))]
    in_specs += [extra_specs(kind, ap, tm, True) for kind, ap, _ in extras]
    in_specs += [extra_specs(kind, asm, s, False) for kind, _, asm in extras]
    operands = [a_p, a_s, w] + [ap for _, ap, _ in extras] + [asm for _, _, asm in extras]
    scratch = [pltpu.VMEM((steps, tk, tn), BF16)]
    if nk > 1:
        scratch += [pltpu.VMEM((tm, tn), F32), pltpu.VMEM((s, tn), F32)]
    return pl.pallas_call(
        functools.partial(_proj_kernel, epilogue=epilogue, n_extra=len(extras), nk=nk,
                          row_chunk=row_chunk),
        grid=(n_tiles // n_sub, m // tm, steps),
        in_specs=in_specs,
        out_specs=[pl.BlockSpec((tm, tn), lambda g, i, t: (i, col(g, i, t))),
                   pl.BlockSpec((s, tn), lambda g, i, t: (0, col_first_rows_only(g, i, t)))],
        out_shape=[jax.ShapeDtypeStruct((m, n_out), dtype_p),
                   jax.ShapeDtypeStruct((s, n_out), dtype_s)],
        scratch_shapes=scratch,
        compiler_params=_params("parallel", "arbitrary", "arbitrary"),
        name=name,
    )(*operands)


def _moba_prompt_kernel(q_ref, k_ref, v_ref, g_ref, o_ref, k16_ref, vt16_ref, *, scale):
    seq, hd = q_ref.shape
    blk, topk = MOBA_BLOCK, MOBA_TOPK
    nb = seq // blk
    k16_ref[...] = k_ref[...].astype(BF16)
    vt16_ref[...] = v_ref[...].T.astype(BF16)
    km_rows = SUBLANES * max(1, -(-(nb - 1) // SUBLANES))
    row_id = lax.broadcasted_iota(jnp.int32, (km_rows, hd), 0)
    kmean = jnp.zeros((km_rows, hd), F32)
    for n in range(nb - 1):
        km = jnp.mean(k_ref[n * blk:(n + 1) * blk, :], axis=0, keepdims=True)
        kmean = jnp.where(row_id == n, km, kmean)
    kmean16 = kmean.astype(BF16)
    key_id = lax.broadcasted_iota(jnp.int32, (blk, blk), 0)
    qry_id = lax.broadcasted_iota(jnp.int32, (blk, blk), 1)
    causal_bias = jnp.where(key_id <= qry_id, 0.0, NEG_INF).astype(F32)
    c = scale * math.log2(math.e)

    def scores(j):
        qj = q_ref[j * blk:(j + 1) * blk, :]
        st = lax.dot_general(k16_ref[0:(j + 1) * blk, :], qj, _NT,
                             preferred_element_type=F32)
        gt = None
        if j > topk:
            gt = lax.dot_general(kmean16, qj, _NT, preferred_element_type=F32)
        return st, gt

    def softmax(j, st, gt):
        biases = None
        if j > topk:
            gates = [gt[n:n + 1, :] for n in range(j)]
            ranks = [jnp.zeros((1, blk), F32) for _ in range(j)]
            for lo in range(j):
                for hi in range(lo + 1, j):
                    lo_ahead = jnp.where(gates[lo] >= gates[hi], 1.0, 0.0)
                    ranks[hi] = ranks[hi] + lo_ahead
                    ranks[lo] = ranks[lo] + (1.0 - lo_ahead)
            biases = [jnp.where(r < topk, 0.0, NEG_INF).astype(F32) for r in ranks]
        pieces = []
        for n in range(j + 1):
            sn = st[n * blk:(n + 1) * blk, :]
            if n == j:
                sn = sn + causal_bias
            elif biases is not None:
                sn = sn + biases[n]
            pieces.append(sn)
        mx = jnp.max(pieces[0], axis=0, keepdims=True)
        for sn in pieces[1:]:
            mx = jnp.maximum(mx, jnp.max(sn, axis=0, keepdims=True))
        probs = [jnp.exp2((sn - mx) * c) for sn in pieces]
        denom = jnp.sum(probs[0], axis=0, keepdims=True)
        for p in probs[1:]:
            denom = denom + jnp.sum(p, axis=0, keepdims=True)
        return jnp.concatenate(probs, axis=0).astype(BF16), denom

    def finish(j, pt16, denom):
        rows = slice(j * blk, (j + 1) * blk)
        ot = jnp.dot(vt16_ref[:, 0:(j + 1) * blk], pt16, preferred_element_type=F32) / denom
        o_ref[rows, :] = (ot.T * _silu(g_ref[rows, :].astype(F32))).astype(o_ref.dtype)

    nxt = scores(0)
    pending = None
    for j in range(nb):
        cur = nxt
        if j + 1 < nb:
            nxt = scores(j + 1)
        if pending is not None:
            finish(*pending)
        pending = (j,) + softmax(j, *cur)
    finish(*pending)


def _moba_prompt(q, k, v, g, batch, seq, hd):
    m, width = q.shape
    heads = width // hd
    assert seq % MOBA_BLOCK == 0 and hd == LANES
    spec = pl.BlockSpec((seq, hd), lambda b, h: (b, h))
    return pl.pallas_call(
        functools.partial(_moba_prompt_kernel, scale=hd ** -0.5),
        grid=(batch, heads),
        in_specs=[spec, spec, spec, spec],
        out_specs=spec,
        out_shape=jax.ShapeDtypeStruct((m, width), BF16),
        scratch_shapes=[pltpu.VMEM((seq, hd), BF16), pltpu.VMEM((hd, seq), BF16)],
        compiler_params=_params("parallel", "parallel"),
        name="moba_prompt",
    )(q, k, v, g)


def _kmean_kernel(pt_ref, *refs, ppb, inv_count):
    *page_refs, o_ref = refs
    for blk in range(len(page_refs) // ppb):
        tot = jnp.sum(page_refs[blk * ppb][...], axis=0)
        for r in page_refs[blk * ppb + 1:(blk + 1) * ppb]:
            tot = tot + jnp.sum(r[...], axis=0)
        o_ref[blk] = tot * inv_count


def _paged_block_means(cache, layer, page_table):
    _, _, page, heads, hd = cache.shape
    db, n_pages = page_table.shape
    ppb = MOBA_BLOCK // page
    assert n_pages % ppb == 0
    nblk = n_pages // ppb
    bps = _tile(nblk, KMEAN_BLOCKS_PER_STEP)
    pps = bps * ppb

    def page_spec(r):
        return pl.BlockSpec((None, None, page, heads, hd),
                            lambda b, n, pt: (layer, pt[b * n_pages + n * pps + r], 0, 0, 0))

    return pl.pallas_call(
        functools.partial(_kmean_kernel, ppb=ppb, inv_count=1.0 / MOBA_BLOCK),
        grid_spec=pltpu.PrefetchScalarGridSpec(
            num_scalar_prefetch=1,
            grid=(db, nblk // bps),
            in_specs=[page_spec(r) for r in range(pps)],
            out_specs=pl.BlockSpec((None, bps, heads, hd), lambda b, n, pt: (b, n, 0, 0)),
        ),
        out_shape=jax.ShapeDtypeStruct((db, nblk, heads, hd), F32),
        compiler_params=_params("parallel", "arbitrary"),
        name="paged_block_means",
    )(page_table.reshape(-1), *([cache] * pps))


def _gate_topk_kernel(q_ref, km_ref, o_ref, *, heads, hd):
    q = q_ref[...].astype(BF16).astype(F32)
    km = km_ref[...].astype(BF16).astype(F32)
    prod = km * q
    nblk = km.shape[0]
    lane = lax.broadcasted_iota(jnp.int32, (nblk, LANES), 1)
    gate = jnp.full((nblk, LANES), NEG_INF, F32)
    for h in range(heads):
        gh = jnp.sum(prod[:, h * hd:(h + 1) * hd], axis=1, keepdims=True)
        gate = jnp.where(lane == h, gh, gate)
    rowf = lax.broadcasted_iota(jnp.int32, (nblk, LANES), 0).astype(F32)
    out_row = lax.broadcasted_iota(jnp.int32, o_ref.shape, 0)
    out = jnp.zeros(o_ref.shape, F32)
    for t in range(MOBA_TOPK):
        mx = jnp.max(gate, axis=0, keepdims=True)
        idx = jnp.min(jnp.where(gate == mx, rowf, float(nblk)), axis=0, keepdims=True)
        out = jnp.where(out_row == t, idx, out)
        gate = jnp.where(rowf == idx, NEG_INF, gate)
    o_ref[...] = out.astype(jnp.int32)


def _gate_topk(q, kmean, heads, hd):
    db, nblk, width = kmean.shape
    assert heads <= LANES and nblk >= MOBA_TOPK
    return pl.pallas_call(
        functools.partial(_gate_topk_kernel, heads=heads, hd=hd),
        grid=(db,),
        in_specs=[pl.BlockSpec((None, 1, width), lambda b: (b, 0, 0)),
                  pl.BlockSpec((None, nblk, width), lambda b: (b, 0, 0))],
        out_specs=pl.BlockSpec((None, SUBLANES, LANES), lambda b: (b, 0, 0)),
        out_shape=jax.ShapeDtypeStruct((db, SUBLANES, LANES), jnp.int32),
        compiler_params=_params("parallel"),
        name="gate_topk",
    )(q, kmean)


def _moba_decode_kernel(sel_ref, pt_ref, q_ref, kn_ref, vn_ref, g_ref, ck_hbm, cv_hbm, o_ref,
                        kbuf, vbuf, sem, *, scale, layer, n_pages, ppb):
    b, h = pl.program_id(0), pl.program_id(1)
    heads = pl.num_programs(1)
    step = b * heads + h
    n_steps = pl.num_programs(0) * heads
    slot = step % 2
    n = kbuf.shape[1]

    def copies(bb, hh, sl):
        out = []
        for t in range(MOBA_TOPK):
            blk = sel_ref[(bb * heads + hh) * MOBA_TOPK + t]
            for r in range(ppb):
                page = pt_ref[bb * n_pages + blk * ppb + r]
                i = t * ppb + r
                out.append(pltpu.make_async_copy(ck_hbm.at[layer, page, :, hh, :],
                                                 kbuf.at[sl, i], sem.at[sl]))
                out.append(pltpu.make_async_copy(cv_hbm.at[layer, page, :, hh, :],
                                                 vbuf.at[sl, i], sem.at[sl]))
        return out

    @pl.when(step == 0)
    def _():
        for cp in copies(b, h, slot):
            cp.start()

    @pl.when(step + 1 < n_steps)
    def _():
        nxt = step + 1
        for cp in copies(nxt // heads, nxt % heads, 1 - slot):
            cp.start()

    for cp in copies(b, h, slot):
        cp.wait()

    hd = q_ref.shape[-1]
    q = q_ref[...]
    q16 = jnp.broadcast_to(q, (BF16_SUBLANES, hd)).astype(BF16)
    scores = [lax.dot_general(q16, kbuf[slot, i].astype(BF16), _NT,
                              preferred_element_type=F32)[0:1, :] * scale for i in range(n)]
    qb = q.astype(BF16).astype(F32)
    s_new = jnp.sum(qb * kn_ref[...].astype(BF16).astype(F32), axis=1, keepdims=True) * scale
    mx = s_new
    for s in scores:
        mx = jnp.maximum(mx, jnp.max(s, axis=1, keepdims=True))
    p_new = jnp.exp(s_new - mx)
    denom = p_new
    acc = p_new.astype(BF16).astype(F32) * vn_ref[...].astype(BF16).astype(F32)
    for i, s in enumerate(scores):
        p = jnp.exp(s - mx)
        denom = denom + jnp.sum(p, axis=1, keepdims=True)
        p16 = jnp.broadcast_to(p, (BF16_SUBLANES, p.shape[1])).astype(BF16)
        acc = acc + jnp.dot(p16, vbuf[slot, i].astype(BF16), preferred_element_type=F32)[0:1, :]
    o = acc / denom
    o_ref[...] = (o * _silu(g_ref[...])).astype(o_ref.dtype)


def _moba_decode(q, k_new, v_new, g, cache_k, cache_v, layer, sel, page_table):
    db, _, width = q.shape
    _, _, page, heads, hd = cache_k.shape
    n_pages = page_table.shape[1]
    ppb = MOBA_BLOCK // page
    n = MOBA_TOPK * ppb
    row_spec = pl.BlockSpec((None, 1, hd), lambda b, h, sel, pt: (b, 0, h))
    hbm_spec = pl.BlockSpec(memory_space=pl.ANY)
    return pl.pallas_call(
        functools.partial(_moba_decode_kernel, scale=hd ** -0.5, layer=layer, n_pages=n_pages,
                          ppb=ppb),
        grid_spec=pltpu.PrefetchScalarGridSpec(
            num_scalar_prefetch=2,
            grid=(db, heads),
            in_specs=[row_spec] * 4 + [hbm_spec, hbm_spec],
            out_specs=row_spec,
            scratch_shapes=[pltpu.VMEM((2, n, page, hd), F32), pltpu.VMEM((2, n, page, hd), F32),
                            pltpu.SemaphoreType.DMA((2,))],
        ),
        out_shape=jax.ShapeDtypeStruct((db, 1, width), BF16),
        compiler_params=_params("arbitrary", "arbitrary"),
        name="moba_decode",
    )(sel, page_table.reshape(-1), q, k_new, v_new, g, cache_k, cache_v)


def _head_norm_gate(o, hg, g):
    ms = jnp.mean(o * o, axis=-1, keepdims=True)
    return (o * lax.rsqrt(ms + EPS) * hg) * _silu(g)


def _ret_prompt_kernel(q_ref, k_ref, v_ref, g_ref, dm_ref, qd_ref, kd_ref, cd_ref, hg_ref,
                       o_ref, s_ref):
    c = dm_ref.shape[-1]
    n_chunks = q_ref.shape[0] // c
    s_ref[...] = jnp.zeros(s_ref.shape, F32)
    dm, qdec, kdec, cdec, hg = dm_ref[...], qd_ref[...], kd_ref[...], cd_ref[...], hg_ref[...]

    def local(ci):
        r = slice(ci * c, (ci + 1) * c)
        qc, kc, vc = q_ref[r, :], k_ref[r, :], v_ref[r, :]
        att = lax.dot_general(qc, kc, _NT, preferred_element_type=F32) * dm
        inner = jnp.dot(att.astype(BF16), vc, preferred_element_type=F32)
        qd = (qc.astype(F32) * qdec).astype(BF16)
        kd = (kc.astype(F32) * kdec).astype(BF16)
        return inner, qd, lax.dot_general(kd, vc, _TN, preferred_element_type=F32)

    def carry(ci, inner, qd, kv):
        r = slice(ci * c, (ci + 1) * c)
        s = s_ref[...]
        o = inner + jnp.dot(qd, s.astype(BF16), preferred_element_type=F32)
        s_ref[...] = s * cdec + kv
        o_ref[r, :] = _head_norm_gate(o, hg, g_ref[r, :].astype(F32)).astype(o_ref.dtype)

    nxt = local(0)
    for ci in range(n_chunks):
        cur = nxt
        if ci + 1 < n_chunks:
            nxt = local(ci + 1)
        carry(ci, *cur)


def _ret_tables(heads, c, dk, dv):
    log_g = jnp.log1p(-jnp.exp2(-5.0 - jnp.arange(heads, dtype=F32)))
    i = jnp.arange(c, dtype=F32)
    diff = i[:, None] - i[None, :]
    dmask = jnp.where(diff >= 0, jnp.exp(log_g[:, None, None] * jnp.maximum(diff, 0.0)), 0.0)
    q_dec = jnp.exp(log_g[:, None] * (i[None, :] + 1.0))
    k_dec = jnp.exp(log_g[:, None] * (c - 1.0 - i[None, :]))
    chunk_dec = jnp.exp(log_g * c)
    return (dmask,
            jnp.broadcast_to(q_dec[:, :, None], (heads, c, dk)),
            jnp.broadcast_to(k_dec[:, :, None], (heads, c, dk)),
            jnp.broadcast_to(chunk_dec[:, None, None], (heads, 1, dv)))


def _ret_prompt(q, k, v, g, head_g, batch, seq, heads):
    m = q.shape[0]
    dk, dv = q.shape[1] // heads, v.shape[1] // heads
    c = RET_CHUNK
    assert seq % c == 0
    dmask, qdec, kdec, cdec = _ret_tables(heads, c, dk, dv)
    qk_spec = pl.BlockSpec((seq, dk), lambda b, h: (b, h))
    v_spec = pl.BlockSpec((seq, dv), lambda b, h: (b, h))

    def head_spec(*shape):
        return pl.BlockSpec((None,) + shape, lambda b, h: (h, 0, 0))

    return pl.pallas_call(
        _ret_prompt_kernel,
        grid=(batch, heads),
        in_specs=[qk_spec, qk_spec, v_spec, v_spec, head_spec(c, c), head_spec(c, dk),
                  head_spec(c, dk), head_spec(1, dv), pl.BlockSpec((1, dv), lambda b, h: (0, 0))],
        out_specs=[v_spec, pl.BlockSpec((None, None, dk, dv), lambda b, h: (b, h, 0, 0))],
        out_shape=[jax.ShapeDtypeStruct((m, heads * dv), BF16),
                   jax.ShapeDtypeStruct((batch, heads, dk, dv), F32)],
        compiler_params=_params("parallel", "parallel"),
        name="ret_prompt",
    )(q, k, v, g, dmask, qdec, kdec, cdec, head_g.reshape(1, dv))


def _ret_decode_kernel(q_ref, k_ref, kcol_ref, v_ref, g_ref, s0_ref, gam_ref, hg_ref,
                       o_ref, s_ref):
    q, k, v, s0, gam = q_ref[...], k_ref[...], v_ref[...], s0_ref[...], gam_ref[...]
    dk = q.shape[-1]
    att = jnp.sum(q.astype(BF16).astype(F32) * k.astype(BF16).astype(F32), axis=1, keepdims=True)
    qd = jnp.broadcast_to(q * gam[:, :dk], (BF16_SUBLANES, dk)).astype(BF16)
    o = att * v + jnp.dot(qd, s0.astype(BF16), preferred_element_type=F32)[0:1, :]
    s_ref[...] = s0 * gam + kcol_ref[...] * v
    o_ref[...] = _head_norm_gate(o, hg_ref[...], g_ref[...]).astype(o_ref.dtype)


def _ret_decode(q, k, v, g, s0, head_g, heads):
    db = q.shape[0]
    dk, dv = q.shape[2] // heads, v.shape[2] // heads
    assert dv >= dk
    gamma = jnp.exp(jnp.log1p(-jnp.exp2(-5.0 - jnp.arange(heads, dtype=F32))) * 1.0)
    gam = jnp.broadcast_to(gamma[:, None, None], (heads, 1, dv))
    kcol = k.reshape(db, heads, dk, 1)
    qk_spec = pl.BlockSpec((None, 1, dk), lambda b, h: (b, 0, h))
    v_spec = pl.BlockSpec((None, 1, dv), lambda b, h: (b, 0, h))
    s_spec = pl.BlockSpec((None, None, dk, dv), lambda b, h: (b, h, 0, 0))
    return pl.pallas_call(
        _ret_decode_kernel,
        grid=(db, heads),
        in_specs=[qk_spec, qk_spec,
                  pl.BlockSpec((None, None, dk, 1), lambda b, h: (b, h, 0, 0)),
                  v_spec, v_spec, s_spec,
                  pl.BlockSpec((None, 1, dv), lambda b, h: (h, 0, 0)),
                  pl.BlockSpec((1, dv), lambda b, h: (0, 0))],
        out_specs=[v_spec, s_spec],
        out_shape=[jax.ShapeDtypeStruct((db, 1, heads * dv), BF16),
                   jax.ShapeDtypeStruct(s0.shape, F32)],
        compiler_params=_params("parallel", "parallel"),
        name="ret_decode",
    )(q, k, kcol, v, g, s0, gam, head_g.reshape(1, dv))


def _rope_tables(pos, dim):
    half = dim // 2
    inv = ROPE_THETA ** (-jnp.arange(half, dtype=F32) / half)
    ang = pos.astype(F32)[:, None] * inv[None, :]
    return jnp.cos(ang), jnp.sin(ang)


def _moba_in_proj(h_p, h_s, w, qn, kn, pos_p, pos_s, width, hd):
    def tables(pos):
        cos, sin = _rope_tables(pos, hd)
        return jnp.concatenate([cos, cos], axis=-1), jnp.concatenate([-sin, sin], axis=-1)

    (cos_p, sin_p), (cos_s, sin_s) = tables(pos_p), tables(pos_s)
    rope = [("rows", cos_p, cos_s), ("rows", sin_p, sin_s)]
    qn, kn = qn.reshape(1, hd), kn.reshape(1, hd)
    q = _proj(h_p, h_s, w, 0, width, BF16, F32, _epi_headnorm_rope,
              [("const", qn, qn)] + rope, name="attn_q_proj")
    k = _proj(h_p, h_s, w, width, width, F32, F32, _epi_headnorm_rope,
              [("const", kn, kn)] + rope, name="attn_k_proj")
    v = _proj(h_p, h_s, w, 2 * width, width, F32, F32, name="attn_v_proj", row_chunk=1024)
    g = _proj(h_p, h_s, w, 3 * width, width, BF16, F32, name="attn_g_proj", row_chunk=512)
    return q, k, v, g


def _ret_in_proj(h_p, h_s, w, pos_p, pos_s, qk_width, v_width, dk):
    (cos_p, sin_p), (cos_s, sin_s) = _rope_tables(pos_p, dk), _rope_tables(pos_s, dk)
    rope = [("rows", cos_p, cos_s), ("rows", sin_p, sin_s)]
    q = _proj(h_p, h_s, w, 0, qk_width, BF16, F32,
              functools.partial(_epi_rope_wide, scale=1.0), rope, name="ret_q_proj")
    k = _proj(h_p, h_s, w, qk_width, qk_width, BF16, F32,
              functools.partial(_epi_rope_wide, scale=dk ** -0.5), rope, name="ret_k_proj")
    v = _proj(h_p, h_s, w, 2 * qk_width, v_width, BF16, F32, name="ret_v_proj")
    g = _proj(h_p, h_s, w, 2 * qk_width + v_width, v_width, BF16, F32, name="ret_g_proj",
              row_chunk=1024)
    return q, k, v, g


def _residual_proj(a_p, a_s, w, x_p, x_s, name, row_chunk=PROJ_ROW_CHUNK):
    return _proj(a_p, a_s, w, 0, w.shape[1], F32, F32, _epi_residual, [("tile", x_p, x_s)],
                 name=name, row_chunk=row_chunk)


def kernel(x_prompt, x_sample, cache_k, cache_v, state_ret, page_table, attn_norm, w_in_attn,
           q_norm, k_norm, w_out_attn, ret_norm, w_in_ret, ret_head_norm, w_out_ret):
    batch, seq, d = x_prompt.shape
    db, dec_seq, _ = x_sample.shape
    assert dec_seq == 1
    n_attn, n_ret = attn_norm.shape[0], ret_norm.shape[0]
    hd = q_norm.shape[-1]
    width = w_out_attn.shape[1]
    heads = width // hd
    dv = ret_head_norm.shape[-1]
    v_width = w_out_ret.shape[1]
    ret_heads = v_width // dv
    qk_width = (w_in_ret.shape[-1] - 2 * v_width) // 2
    dk = qk_width // ret_heads
    assert dk == 2 * LANES
    n_pages = page_table.shape[1]
    past_len = n_pages * cache_k.shape[2]
    assert cache_k.shape[2] == PAGE_SIZE and past_len % MOBA_BLOCK == 0
    assert cache_k.shape[3:] == (heads, hd)

    m = batch * seq
    sp = BF16_SUBLANES * (-(-db // BF16_SUBLANES))
    xp = x_prompt.reshape(m, d)
    xs = jnp.pad(x_sample.reshape(db, d), ((0, sp - db), (0, 0)))
    pos_p = jnp.arange(seq, dtype=jnp.int32)
    pos_s = jnp.full((sp,), past_len, jnp.int32)

    def pad_rows(t):
        return jnp.pad(t.reshape(db, -1), ((0, sp - db), (0, 0)))

    kp_l, vp_l, ks_l, vs_l, sp_l, ss_l = [], [], [], [], [], []
    for layer in range(n_attn + n_ret):
        j = layer // 2
        if layer % 2 == 0:
            h_p, h_s = _rmsnorm_bf16(xp, attn_norm[j]), _rmsnorm_bf16(xs, attn_norm[j])
            (q_p, q_s), (k_p, k_s), (v_p, v_s), (g_p, g_s) = _moba_in_proj(
                h_p, h_s, w_in_attn[j], q_norm[j], k_norm[j], pos_p, pos_s, width, hd)
            o_p = _moba_prompt(q_p, k_p, v_p, g_p, batch, seq, hd)
            q_s, k_s, v_s, g_s = (t[:db].reshape(db, 1, width) for t in (q_s, k_s, v_s, g_s))
            kmean = _paged_block_means(cache_k, j, page_table)
            kmean = kmean.reshape(db, kmean.shape[1], width)
            sel = _gate_topk(q_s, kmean, heads, hd)[:, :MOBA_TOPK, :heads]
            sel = jnp.swapaxes(sel, 1, 2).reshape(-1)
            o_s = _moba_decode(q_s, k_s, v_s, g_s, cache_k, cache_v, j, sel, page_table)
            xp, xs = _residual_proj(o_p, pad_rows(o_s), w_out_attn[j], xp, xs, "attn_out_proj")
            kp_l.append(k_p.reshape(batch, seq, heads, hd))
            vp_l.append(v_p.reshape(batch, seq, heads, hd))
            ks_l.append(k_s.reshape(db, 1, heads, hd))
            vs_l.append(v_s.reshape(db, 1, heads, hd))
        else:
            h_p, h_s = _rmsnorm_bf16(xp, ret_norm[j]), _rmsnorm_bf16(xs, ret_norm[j])
            (q_p, q_s), (k_p, k_s), (v_p, v_s), (g_p, g_s) = _ret_in_proj(
                h_p, h_s, w_in_ret[j], pos_p, pos_s, qk_width, v_width, dk)
            o_p, s_p = _ret_prompt(q_p, k_p, v_p, g_p, ret_head_norm[j], batch, seq, ret_heads)
            q_s, k_s, v_s, g_s = (t[:db].reshape(db, 1, -1) for t in (q_s, k_s, v_s, g_s))
            o_s, s_s = _ret_decode(q_s, k_s, v_s, g_s, state_ret[j], ret_head_norm[j], ret_heads)
            xp, xs = _residual_proj(o_p, pad_rows(o_s), w_out_ret[j], xp, xs, "ret_out_proj",
                                    row_chunk=1024)
            sp_l.append(s_p)
            ss_l.append(s_s)

    return (xp.reshape(batch, seq, d), xs[:db].reshape(db, 1, d),
            jnp.stack(kp_l), jnp.stack(vp_l), jnp.stack(ks_l), jnp.stack(vs_l),
            jnp.stack(sp_l), jnp.stack(ss_l))
```

```python
import functools
import math

import jax
import jax.numpy as jnp
from jax import lax
from jax.experimental import pallas as pl
from jax.experimental.pallas import tpu as pltpu

EPS = 1e-6
PAGE_SIZE = 128
MOBA_BLOCK = 256
MOBA_TOPK = 3
ROPE_THETA = 10000.0
RET_CHUNK = 128

F32 = jnp.float32
BF16 = jnp.bfloat16
NEG_INF = float("-inf")

LANES = 128
SUBLANES = 8
BF16_SUBLANES = 16
V7X_VMEM_BYTES = 64 * 1024 * 1024
VMEM_LIMIT_BYTES = V7X_VMEM_BYTES * 7 // 8

PROJ_TM = 1024
PROJ_TN = 512
PROJ_TK = 4096
PROJ_DMA_PARTS = 4
PROJ_SUBTILES = 2
KMEAN_BLOCKS_PER_STEP = 2

_NT = (((1,), (1,)), ((), ()))
_TN = (((0,), (0,)), ((), ()))


def _params(*sem):
    return pltpu.CompilerParams(dimension_semantics=sem, vmem_limit_bytes=VMEM_LIMIT_BYTES)


def _tile(n, target):
    if n <= target:
        return n
    t = target
    while n % t:
        t -= LANES if t > LANES else 1
    return t


def _silu(g):
    return g / (1.0 + jnp.exp(-g))


def _rmsnorm_kernel(x_ref, g_ref, o_ref):
    x = x_ref[...]
    ms = jnp.mean(x * x, axis=-1, keepdims=True)
    o_ref[...] = (x * lax.rsqrt(ms + EPS) * g_ref[...]).astype(o_ref.dtype)


def _rmsnorm_bf16(x, g):
    m, d = x.shape
    tm = _tile(m, 256)
    return pl.pallas_call(
        _rmsnorm_kernel,
        grid=(m // tm,),
        in_specs=[pl.BlockSpec((tm, d), lambda i: (i, 0)),
                  pl.BlockSpec((1, d), lambda i: (0, 0))],
        out_specs=pl.BlockSpec((tm, d), lambda i: (i, 0)),
        out_shape=jax.ShapeDtypeStruct((m, d), BF16),
        compiler_params=_params("parallel"),
        name="rmsnorm",
    )(x, g.reshape(1, d))


def _epi_plain(acc, rows):
    return acc


def _epi_residual(acc, rows, x_ref):
    return x_ref[rows, :] + acc


def _epi_headnorm_rope(acc, rows, gn_ref, cos_ref, sin_ref):
    hd = gn_ref.shape[-1]
    gn, cos2, sin2 = gn_ref[...], cos_ref[rows, :], sin_ref[rows, :]
    outs = []
    for c in range(acc.shape[1] // hd):
        z = acc[:, c * hd:(c + 1) * hd]
        ms = jnp.mean(z * z, axis=-1, keepdims=True)
        zn = z * lax.rsqrt(ms + EPS) * gn
        outs.append(zn * cos2 + pltpu.roll(zn, hd // 2, 1) * sin2)
    return jnp.concatenate(outs, axis=1)


def _epi_rope_wide(acc, rows, cos_ref, sin_ref, *, scale):
    half = cos_ref.shape[-1]
    cos, sin = cos_ref[rows, :], sin_ref[rows, :]
    outs = []
    for c in range(acc.shape[1] // (2 * half)):
        x1 = acc[:, (2 * c) * half:(2 * c + 1) * half]
        x2 = acc[:, (2 * c + 1) * half:(2 * c + 2) * half]
        outs.append((x1 * cos - x2 * sin) * scale)
        outs.append((x2 * cos + x1 * sin) * scale)
    return jnp.concatenate(outs, axis=1)


def _proj_kernel(*refs, epilogue, n_extra, nk, n_parts):
    ap_refs, as_ref, w_refs = refs[:n_parts], refs[n_parts], refs[n_parts + 1:2 * n_parts + 1]
    refs = refs[2 * n_parts + 1:]
    extras_p = refs[:n_extra]
    extras_s = refs[n_extra:2 * n_extra]
    op_ref, os_ref, wbf_ref = refs[2 * n_extra:2 * n_extra + 3]
    acc_refs = refs[2 * n_extra + 3:]
    i, t = pl.program_id(1), pl.program_id(2)

    def tile(a_ref, rows, acc_ref, o_ref, extras):
        acc = jnp.dot(a_ref[...], wbf_ref[t], preferred_element_type=F32)
        if nk == 1:
            o_ref[rows, :] = epilogue(acc, rows, *extras).astype(o_ref.dtype)
            return

        @pl.when(t == 0)
        def _():
            acc_ref[rows, :] = acc

        if nk > 2:
            @pl.when(jnp.logical_and(t > 0, t < nk - 1))
            def _():
                acc_ref[rows, :] += acc

        @pl.when(t == nk - 1)
        def _():
            o_ref[rows, :] = epilogue(acc_ref[rows, :] + acc, rows, *extras).astype(o_ref.dtype)

    @pl.when(i == 0)
    def _():
        rows_k = w_refs[0].shape[0]
        for c, w_ref in enumerate(w_refs):
            wbf_ref[t, c * rows_k:(c + 1) * rows_k, :] = w_ref[...].astype(BF16)
        tile(as_ref, slice(None), acc_refs[1] if nk > 1 else None, os_ref, extras_s)

    rows_m = ap_refs[0].shape[0]
    for c, a_ref in enumerate(ap_refs):
        tile(a_ref, slice(c * rows_m, (c + 1) * rows_m), acc_refs[0] if nk > 1 else None, op_ref,
             extras_p)


def _proj(a_p, a_s, w, col0, n_out, dtype_p, dtype_s, epilogue=_epi_plain, extras=(), name="proj"):
    m, k = a_p.shape
    s = a_s.shape[0]
    tm, tn, tk = _tile(m, PROJ_TM), _tile(n_out, PROJ_TN), _tile(k, PROJ_TK)
    nk = k // tk
    n_tiles = n_out // tn
    n_sub = PROJ_SUBTILES if nk == 1 and n_tiles % PROJ_SUBTILES == 0 else 1
    steps = n_sub if nk == 1 else nk
    n_parts = PROJ_DMA_PARTS if tm % PROJ_DMA_PARTS == 0 and tk % PROJ_DMA_PARTS == 0 else 1
    rows_m, rows_k = tm // n_parts, tk // n_parts
    assert col0 % tn == 0
    off = col0 // tn

    def col(g, i, t):
        return g * n_sub + t if nk == 1 else g

    def col_first_rows_only(g, i, t):
        return jnp.where(i == 0, col(g, i, t), g * n_sub + n_sub - 1) if nk == 1 else g

    def kchunk(t):
        return 0 if nk == 1 else t

    def extra_specs(kind, arr, rows_per_tile, is_prompt):
        if kind == "const":
            return pl.BlockSpec(arr.shape, lambda g, i, t: (0,) * arr.ndim)
        if kind == "rows":
            assert arr.shape[0] % rows_per_tile == 0
            period = arr.shape[0] // rows_per_tile
            if is_prompt:
                return pl.BlockSpec((rows_per_tile, arr.shape[1]), lambda g, i, t: (i % period, 0))
            return pl.BlockSpec((rows_per_tile, arr.shape[1]), lambda g, i, t: (0, 0))
        assert kind == "tile"
        if is_prompt:
            return pl.BlockSpec((rows_per_tile, tn), lambda g, i, t: (i, col(g, i, t)))
        return pl.BlockSpec((rows_per_tile, tn), lambda g, i, t: (0, col_first_rows_only(g, i, t)))

    def a_spec(c):
        return pl.BlockSpec((rows_m, tk), lambda g, i, t: (i * n_parts + c, kchunk(t)))

    def w_spec(c):
        return pl.BlockSpec((rows_k, tn), lambda g, i, t: (kchunk(t) * n_parts + c,
                                                           off + col_first_rows_only(g, i, t)))

    in_specs = ([a_spec(c) for c in range(n_parts)]
                + [pl.BlockSpec((s, tk), lambda g, i, t: (0, kchunk(t)))]
                + [w_spec(c) for c in range(n_parts)])
    in_specs += [extra_specs(kind, ap, tm, True) for kind, ap, _ in extras]
    in_specs += [extra_specs(kind, asm, s, False) for kind, _, asm in extras]
    operands = ([a_p] * n_parts + [a_s] + [w] * n_parts + [ap for _, ap, _ in extras]
                + [asm for _, _, asm in extras])
    scratch = [pltpu.VMEM((steps, tk, tn), BF16)]
    if nk > 1:
        scratch += [pltpu.VMEM((tm, tn), F32), pltpu.VMEM((s, tn), F32)]
    return pl.pallas_call(
        functools.partial(_proj_kernel, epilogue=epilogue, n_extra=len(extras), nk=nk,
                          n_parts=n_parts),
        grid=(n_tiles // n_sub, m // tm, steps),
        in_specs=in_specs,
        out_specs=[pl.BlockSpec((tm, tn), lambda g, i, t: (i, col(g, i, t))),
                   pl.BlockSpec((s, tn), lambda g, i, t: (0, col_first_rows_only(g, i, t)))],
        out_shape=[jax.ShapeDtypeStruct((m, n_out), dtype_p),
                   jax.ShapeDtypeStruct((s, n_out), dtype_s)],
        scratch_shapes=scratch,
        compiler_params=_params("parallel", "arbitrary", "arbitrary"),
        name=name,
    )(*operands)


def _block_means(page_refs, o_ref, ppb):
    for blk in range(len(page_refs) // ppb):
        tot = jnp.sum(page_refs[blk * ppb][...], axis=0)
        for r in page_refs[blk * ppb + 1:(blk + 1) * ppb]:
            tot = tot + jnp.sum(r[...], axis=0)
        o_ref[blk] = tot * (1.0 / MOBA_BLOCK)


def _moba_prompt_kernel(*refs, scale, n_pages_in, ppb):
    if n_pages_in:
        refs = refs[1:]
    q_ref, k_ref, v_ref, g_ref = refs[:4]
    page_refs = refs[4:4 + n_pages_in]
    o_ref = refs[4 + n_pages_in]
    k16_ref, vt16_ref = refs[-2:]
    if n_pages_in:
        _block_means(page_refs, refs[5 + n_pages_in], ppb)
    seq, hd = q_ref.shape
    blk, topk = MOBA_BLOCK, MOBA_TOPK
    nb = seq // blk
    k16_ref[...] = k_ref[...].astype(BF16)
    vt16_ref[...] = v_ref[...].T.astype(BF16)
    km_rows = SUBLANES * max(1, -(-(nb - 1) // SUBLANES))
    row_id = lax.broadcasted_iota(jnp.int32, (km_rows, hd), 0)
    kmean = jnp.zeros((km_rows, hd), F32)
    for n in range(nb - 1):
        km = jnp.mean(k_ref[n * blk:(n + 1) * blk, :], axis=0, keepdims=True)
        kmean = jnp.where(row_id == n, km, kmean)
    kmean16 = kmean.astype(BF16)
    key_id = lax.broadcasted_iota(jnp.int32, (blk, blk), 0)
    qry_id = lax.broadcasted_iota(jnp.int32, (blk, blk), 1)
    causal_bias = jnp.where(key_id <= qry_id, 0.0, NEG_INF).astype(F32)
    c = scale * math.log2(math.e)

    def scores(j):
        qj = q_ref[j * blk:(j + 1) * blk, :]
        st = lax.dot_general(k16_ref[0:(j + 1) * blk, :], qj, _NT,
                             preferred_element_type=F32)
        gt = None
        if j > topk:
            gt = lax.dot_general(kmean16, qj, _NT, preferred_element_type=F32)
        return st, gt

    def softmax(j, st, gt):
        biases = None
        if j > topk:
            gates = [gt[n:n + 1, :] for n in range(j)]
            ranks = [jnp.zeros((1, blk), F32) for _ in range(j)]
            for lo in range(j):
                for hi in range(lo + 1, j):
                    lo_ahead = jnp.where(gates[lo] >= gates[hi], 1.0, 0.0)
                    ranks[hi] = ranks[hi] + lo_ahead
                    ranks[lo] = ranks[lo] + (1.0 - lo_ahead)
            biases = [jnp.where(r < topk, 0.0, NEG_INF).astype(F32) for r in ranks]
        pieces = []
        for n in range(j + 1):
            sn = st[n * blk:(n + 1) * blk, :]
            if n == j:
                sn = sn + causal_bias
            elif biases is not None:
                sn = sn + biases[n]
            pieces.append(sn)
        mx = jnp.max(pieces[0], axis=0, keepdims=True)
        for sn in pieces[1:]:
            mx = jnp.maximum(mx, jnp.max(sn, axis=0, keepdims=True))
        probs = [jnp.exp2((sn - mx) * c) for sn in pieces]
        denom = jnp.sum(probs[0], axis=0, keepdims=True)
        for p in probs[1:]:
            denom = denom + jnp.sum(p, axis=0, keepdims=True)
        return jnp.concatenate(probs, axis=0).astype(BF16), denom

    def finish(j, pt16, denom):
        rows = slice(j * blk, (j + 1) * blk)
        ot = jnp.dot(vt16_ref[:, 0:(j + 1) * blk], pt16, preferred_element_type=F32) / denom
        o_ref[rows, :] = (ot.T * _silu(g_ref[rows, :].astype(F32))).astype(o_ref.dtype)

    nxt = scores(0)
    pending = None
    for j in range(nb):
        cur = nxt
        if j + 1 < nb:
            nxt = scores(j + 1)
        if pending is not None:
            finish(*pending)
        pending = (j,) + softmax(j, *cur)
    finish(*pending)


def _moba_prompt(q, k, v, g, batch, seq, hd, cache, layer, page_table):
    m, width = q.shape
    heads = width // hd
    assert seq % MOBA_BLOCK == 0 and hd == LANES
    _, _, page, c_heads, c_hd = cache.shape
    ppb = MOBA_BLOCK // page
    total_pages = page_table.size
    steps = batch * heads
    pps = total_pages // steps
    page_bytes = page * c_heads * c_hd * 4
    own_bytes = seq * hd * (2 * (3 * 2 + 2 * 4) + 2 * 2) + 2 * seq * MOBA_BLOCK * 4
    hosted = (total_pages % steps == 0 and pps % ppb == 0 and page_table.shape[1] % ppb == 0
              and own_bytes + 2 * pps * page_bytes <= VMEM_LIMIT_BYTES)
    if not hosted:
        pps = 0
    bps = pps // ppb
    spec = pl.BlockSpec((seq, hd), lambda b, h, *_: (b, h))

    def page_spec(r):
        return pl.BlockSpec((None, None, page, c_heads, c_hd),
                            lambda b, h, pt: (layer, pt[(b * heads + h) * pps + r], 0, 0, 0))

    out_specs, out_shape = [spec], [jax.ShapeDtypeStruct((m, width), BF16)]
    if hosted:
        out_specs.append(pl.BlockSpec((bps, c_heads, c_hd), lambda b, h, pt: (b * heads + h, 0, 0)))
        out_shape.append(jax.ShapeDtypeStruct((total_pages // ppb, c_heads, c_hd), F32))
    outs = pl.pallas_call(
        functools.partial(_moba_prompt_kernel, scale=hd ** -0.5, n_pages_in=pps, ppb=ppb),
        grid_spec=pltpu.PrefetchScalarGridSpec(
            num_scalar_prefetch=1 if hosted else 0,
            grid=(batch, heads),
            in_specs=[spec] * 4 + [page_spec(r) for r in range(pps)],
            out_specs=out_specs,
            scratch_shapes=[pltpu.VMEM((seq, hd), BF16), pltpu.VMEM((hd, seq), BF16)],
        ),
        out_shape=out_shape,
        compiler_params=_params("parallel", "parallel"),
        name="moba_prompt",
    )(*([page_table.reshape(-1)] if hosted else []), q, k, v, g, *([cache] * pps))
    if hosted:
        return outs[0], outs[1].reshape(page_table.shape[0], -1, c_heads, c_hd)
    return outs[0], None


def _kmean_kernel(pt_ref, *refs, ppb):
    *page_refs, o_ref = refs
    _block_means(page_refs, o_ref, ppb)


def _paged_block_means(cache, layer, page_table):
    _, _, page, heads, hd = cache.shape
    db, n_pages = page_table.shape
    ppb = MOBA_BLOCK // page
    assert n_pages % ppb == 0
    nblk = n_pages // ppb
    bps = _tile(nblk, KMEAN_BLOCKS_PER_STEP)
    pps = bps * ppb

    def page_spec(r):
        return pl.BlockSpec((None, None, page, heads, hd),
                            lambda b, n, pt: (layer, pt[b * n_pages + n * pps + r], 0, 0, 0))

    return pl.pallas_call(
        functools.partial(_kmean_kernel, ppb=ppb),
        grid_spec=pltpu.PrefetchScalarGridSpec(
            num_scalar_prefetch=1,
            grid=(db, nblk // bps),
            in_specs=[page_spec(r) for r in range(pps)],
            out_specs=pl.BlockSpec((None, bps, heads, hd), lambda b, n, pt: (b, n, 0, 0)),
        ),
        out_shape=jax.ShapeDtypeStruct((db, nblk, heads, hd), F32),
        compiler_params=_params("parallel", "arbitrary"),
        name="paged_block_means",
    )(page_table.reshape(-1), *([cache] * pps))


def _gate_topk_kernel(q_ref, km_ref, o_ref, *, heads, hd):
    q = q_ref[...].astype(BF16).astype(F32)
    km = km_ref[...].astype(BF16).astype(F32)
    prod = km * q
    nblk = km.shape[0]
    lane = lax.broadcasted_iota(jnp.int32, (nblk, LANES), 1)
    gate = jnp.full((nblk, LANES), NEG_INF, F32)
    for h in range(heads):
        gh = jnp.sum(prod[:, h * hd:(h + 1) * hd], axis=1, keepdims=True)
        gate = jnp.where(lane == h, gh, gate)
    rowf = lax.broadcasted_iota(jnp.int32, (nblk, LANES), 0).astype(F32)
    out_row = lax.broadcasted_iota(jnp.int32, o_ref.shape, 0)
    out = jnp.zeros(o_ref.shape, F32)
    for t in range(MOBA_TOPK):
        mx = jnp.max(gate, axis=0, keepdims=True)
        idx = jnp.min(jnp.where(gate == mx, rowf, float(nblk)), axis=0, keepdims=True)
        out = jnp.where(out_row == t, idx, out)
        gate = jnp.where(rowf == idx, NEG_INF, gate)
    o_ref[...] = out.astype(jnp.int32)


def _gate_topk(q, kmean, heads, hd):
    db, nblk, width = kmean.shape
    assert heads <= LANES and nblk >= MOBA_TOPK
    return pl.pallas_call(
        functools.partial(_gate_topk_kernel, heads=heads, hd=hd),
        grid=(db,),
        in_specs=[pl.BlockSpec((None, 1, width), lambda b: (b, 0, 0)),
                  pl.BlockSpec((None, nblk, width), lambda b: (b, 0, 0))],
        out_specs=pl.BlockSpec((None, SUBLANES, LANES), lambda b: (b, 0, 0)),
        out_shape=jax.ShapeDtypeStruct((db, SUBLANES, LANES), jnp.int32),
        compiler_params=_params("parallel"),
        name="gate_topk",
    )(q, kmean)


def _moba_decode_kernel(sel_ref, pt_ref, q_ref, kn_ref, vn_ref, g_ref, ck_hbm, cv_hbm, o_ref,
                        kbuf, vbuf, sem, *, scale, layer, n_pages, ppb):
    b, h = pl.program_id(0), pl.program_id(1)
    heads = pl.num_programs(1)
    step = b * heads + h
    n_steps = pl.num_programs(0) * heads
    slot = step % 2
    n = kbuf.shape[1]

    def copies(bb, hh, sl):
        out = []
        for t in range(MOBA_TOPK):
            blk = sel_ref[(bb * heads + hh) * MOBA_TOPK + t]
            for r in range(ppb):
                page = pt_ref[bb * n_pages + blk * ppb + r]
                i = t * ppb + r
                out.append(pltpu.make_async_copy(ck_hbm.at[layer, page, :, hh, :],
                                                 kbuf.at[sl, i], sem.at[sl]))
                out.append(pltpu.make_async_copy(cv_hbm.at[layer, page, :, hh, :],
                                                 vbuf.at[sl, i], sem.at[sl]))
        return out

    @pl.when(step == 0)
    def _():
        for cp in copies(b, h, slot):
            cp.start()

    @pl.when(step + 1 < n_steps)
    def _():
        nxt = step + 1
        for cp in copies(nxt // heads, nxt % heads, 1 - slot):
            cp.start()

    for cp in copies(b, h, slot):
        cp.wait()

    hd = q_ref.shape[-1]
    q = q_ref[...]
    q16 = jnp.broadcast_to(q, (BF16_SUBLANES, hd)).astype(BF16)
    scores = [lax.dot_general(q16, kbuf[slot, i].astype(BF16), _NT,
                              preferred_element_type=F32)[0:1, :] * scale for i in range(n)]
    qb = q.astype(BF16).astype(F32)
    s_new = jnp.sum(qb * kn_ref[...].astype(BF16).astype(F32), axis=1, keepdims=True) * scale
    mx = s_new
    for s in scores:
        mx = jnp.maximum(mx, jnp.max(s, axis=1, keepdims=True))
    p_new = jnp.exp(s_new - mx)
    denom = p_new
    acc = p_new.astype(BF16).astype(F32) * vn_ref[...].astype(BF16).astype(F32)
    for i, s in enumerate(scores):
        p = jnp.exp(s - mx)
        denom = denom + jnp.sum(p, axis=1, keepdims=True)
        p16 = jnp.broadcast_to(p, (BF16_SUBLANES, p.shape[1])).astype(BF16)
        acc = acc + jnp.dot(p16, vbuf[slot, i].astype(BF16), preferred_element_type=F32)[0:1, :]
    o = acc / denom
    o_ref[...] = (o * _silu(g_ref[...])).astype(o_ref.dtype)


def _moba_decode(q, k_new, v_new, g, cache_k, cache_v, layer, sel, page_table):
    db, _, width = q.shape
    _, _, page, heads, hd = cache_k.shape
    n_pages = page_table.shape[1]
    ppb = MOBA_BLOCK // page
    n = MOBA_TOPK * ppb
    row_spec = pl.BlockSpec((None, 1, hd), lambda b, h, sel, pt: (b, 0, h))
    hbm_spec = pl.BlockSpec(memory_space=pl.ANY)
    return pl.pallas_call(
        functools.partial(_moba_decode_kernel, scale=hd ** -0.5, layer=layer, n_pages=n_pages,
                          ppb=ppb),
        grid_spec=pltpu.PrefetchScalarGridSpec(
            num_scalar_prefetch=2,
            grid=(db, heads),
            in_specs=[row_spec] * 4 + [hbm_spec, hbm_spec],
            out_specs=row_spec,
            scratch_shapes=[pltpu.VMEM((2, n, page, hd), F32), pltpu.VMEM((2, n, page, hd), F32),
                            pltpu.SemaphoreType.DMA((2,))],
        ),
        out_shape=jax.ShapeDtypeStruct((db, 1, width), BF16),
        compiler_params=_params("arbitrary", "arbitrary"),
        name="moba_decode",
    )(sel, page_table.reshape(-1), q, k_new, v_new, g, cache_k, cache_v)


def _head_norm_gate(o, hg, g):
    ms = jnp.mean(o * o, axis=-1, keepdims=True)
    return (o * lax.rsqrt(ms + EPS) * hg) * _silu(g)


def _ret_prompt_kernel(q_ref, k_ref, v_ref, g_ref, dm_ref, qd_ref, kd_ref, cd_ref, hg_ref,
                       o_ref, s_ref):
    c = dm_ref.shape[-1]
    n_chunks = q_ref.shape[0] // c
    s_ref[...] = jnp.zeros(s_ref.shape, F32)
    dm, qdec, kdec, cdec, hg = dm_ref[...], qd_ref[...], kd_ref[...], cd_ref[...], hg_ref[...]

    def local(ci):
        r = slice(ci * c, (ci + 1) * c)
        qc, kc, vc = q_ref[r, :], k_ref[r, :], v_ref[r, :]
        att = lax.dot_general(qc, kc, _NT, preferred_element_type=F32) * dm
        inner = jnp.dot(att.astype(BF16), vc, preferred_element_type=F32)
        qd = (qc.astype(F32) * qdec).astype(BF16)
        kd = (kc.astype(F32) * kdec).astype(BF16)
        return inner, qd, lax.dot_general(kd, vc, _TN, preferred_element_type=F32)

    def carry(ci, inner, qd, kv):
        r = slice(ci * c, (ci + 1) * c)
        s = s_ref[...]
        o = inner + jnp.dot(qd, s.astype(BF16), preferred_element_type=F32)
        s_ref[...] = s * cdec + kv
        o_ref[r, :] = _head_norm_gate(o, hg, g_ref[r, :].astype(F32)).astype(o_ref.dtype)

    nxt = local(0)
    for ci in range(n_chunks):
        cur = nxt
        if ci + 1 < n_chunks:
            nxt = local(ci + 1)
        carry(ci, *cur)


def _ret_tables(heads, c, dk, dv):
    log_g = jnp.log1p(-jnp.exp2(-5.0 - jnp.arange(heads, dtype=F32)))
    i = jnp.arange(c, dtype=F32)
    diff = i[:, None] - i[None, :]
    dmask = jnp.where(diff >= 0, jnp.exp(log_g[:, None, None] * jnp.maximum(diff, 0.0)), 0.0)
    q_dec = jnp.exp(log_g[:, None] * (i[None, :] + 1.0))
    k_dec = jnp.exp(log_g[:, None] * (c - 1.0 - i[None, :]))
    chunk_dec = jnp.exp(log_g * c)
    return (dmask,
            jnp.broadcast_to(q_dec[:, :, None], (heads, c, dk)),
            jnp.broadcast_to(k_dec[:, :, None], (heads, c, dk)),
            jnp.broadcast_to(chunk_dec[:, None, None], (heads, 1, dv)))


def _ret_prompt(q, k, v, g, head_g, batch, seq, heads):
    m = q.shape[0]
    dk, dv = q.shape[1] // heads, v.shape[1] // heads
    c = RET_CHUNK
    assert seq % c == 0
    dmask, qdec, kdec, cdec = _ret_tables(heads, c, dk, dv)
    qk_spec = pl.BlockSpec((seq, dk), lambda b, h: (b, h))
    v_spec = pl.BlockSpec((seq, dv), lambda b, h: (b, h))

    def head_spec(*shape):
        return pl.BlockSpec((None,) + shape, lambda b, h: (h, 0, 0))

    return pl.pallas_call(
        _ret_prompt_kernel,
        grid=(batch, heads),
        in_specs=[qk_spec, qk_spec, v_spec, v_spec, head_spec(c, c), head_spec(c, dk),
                  head_spec(c, dk), head_spec(1, dv), pl.BlockSpec((1, dv), lambda b, h: (0, 0))],
        out_specs=[v_spec, pl.BlockSpec((None, None, dk, dv), lambda b, h: (b, h, 0, 0))],
        out_shape=[jax.ShapeDtypeStruct((m, heads * dv), BF16),
                   jax.ShapeDtypeStruct((batch, heads, dk, dv), F32)],
        compiler_params=_params("parallel", "parallel"),
        name="ret_prompt",
    )(q, k, v, g, dmask, qdec, kdec, cdec, head_g.reshape(1, dv))


def _ret_decode_kernel(q_ref, k_ref, kcol_ref, v_ref, g_ref, s0_ref, gam_ref, hg_ref,
                       o_ref, s_ref):
    q, k, v, s0, gam = q_ref[...], k_ref[...], v_ref[...], s0_ref[...], gam_ref[...]
    dk = q.shape[-1]
    att = jnp.sum(q.astype(BF16).astype(F32) * k.astype(BF16).astype(F32), axis=1, keepdims=True)
    qd = jnp.broadcast_to(q * gam[:, :dk], (BF16_SUBLANES, dk)).astype(BF16)
    o = att * v + jnp.dot(qd, s0.astype(BF16), preferred_element_type=F32)[0:1, :]
    s_ref[...] = s0 * gam + kcol_ref[...] * v
    o_ref[...] = _head_norm_gate(o, hg_ref[...], g_ref[...]).astype(o_ref.dtype)


def _ret_decode(q, k, v, g, s0, head_g, heads):
    db = q.shape[0]
    dk, dv = q.shape[2] // heads, v.shape[2] // heads
    assert dv >= dk
    gamma = jnp.exp(jnp.log1p(-jnp.exp2(-5.0 - jnp.arange(heads, dtype=F32))) * 1.0)
    gam = jnp.broadcast_to(gamma[:, None, None], (heads, 1, dv))
    kcol = k.reshape(db, heads, dk, 1)
    qk_spec = pl.BlockSpec((None, 1, dk), lambda b, h: (b, 0, h))
    v_spec = pl.BlockSpec((None, 1, dv), lambda b, h: (b, 0, h))
    s_spec = pl.BlockSpec((None, None, dk, dv), lambda b, h: (b, h, 0, 0))
    return pl.pallas_call(
        _ret_decode_kernel,
        grid=(db, heads),
        in_specs=[qk_spec, qk_spec,
                  pl.BlockSpec((None, None, dk, 1), lambda b, h: (b, h, 0, 0)),
                  v_spec, v_spec, s_spec,
                  pl.BlockSpec((None, 1, dv), lambda b, h: (h, 0, 0)),
                  pl.BlockSpec((1, dv), lambda b, h: (0, 0))],
        out_specs=[v_spec, s_spec],
        out_shape=[jax.ShapeDtypeStruct((db, 1, heads * dv), BF16),
                   jax.ShapeDtypeStruct(s0.shape, F32)],
        compiler_params=_params("parallel", "parallel"),
        name="ret_decode",
    )(q, k, kcol, v, g, s0, gam, head_g.reshape(1, dv))


def _rope_tables(pos, dim):
    half = dim // 2
    inv = ROPE_THETA ** (-jnp.arange(half, dtype=F32) / half)
    ang = pos.astype(F32)[:, None] * inv[None, :]
    return jnp.cos(ang), jnp.sin(ang)


def _moba_in_proj(h_p, h_s, w, qn, kn, pos_p, pos_s, width, hd):
    def tables(pos):
        cos, sin = _rope_tables(pos, hd)
        return jnp.concatenate([cos, cos], axis=-1), jnp.concatenate([-sin, sin], axis=-1)

    (cos_p, sin_p), (cos_s, sin_s) = tables(pos_p), tables(pos_s)
    rope = [("rows", cos_p, cos_s), ("rows", sin_p, sin_s)]
    qn, kn = qn.reshape(1, hd), kn.reshape(1, hd)
    q = _proj(h_p, h_s, w, 0, width, BF16, F32, _epi_headnorm_rope,
              [("const", qn, qn)] + rope, name="attn_q_proj")
    k = _proj(h_p, h_s, w, width, width, F32, F32, _epi_headnorm_rope,
              [("const", kn, kn)] + rope, name="attn_k_proj")
    v = _proj(h_p, h_s, w, 2 * width, width, F32, F32, name="attn_v_proj")
    g = _proj(h_p, h_s, w, 3 * width, width, BF16, F32, name="attn_g_proj")
    return q, k, v, g


def _ret_in_proj(h_p, h_s, w, pos_p, pos_s, qk_width, v_width, dk):
    (cos_p, sin_p), (cos_s, sin_s) = _rope_tables(pos_p, dk), _rope_tables(pos_s, dk)
    rope = [("rows", cos_p, cos_s), ("rows", sin_p, sin_s)]
    q = _proj(h_p, h_s, w, 0, qk_width, BF16, F32,
              functools.partial(_epi_rope_wide, scale=1.0), rope, name="ret_q_proj")
    k = _proj(h_p, h_s, w, qk_width, qk_width, BF16, F32,
              functools.partial(_epi_rope_wide, scale=dk ** -0.5), rope, name="ret_k_proj")
    v = _proj(h_p, h_s, w, 2 * qk_width, v_width, BF16, F32, name="ret_v_proj")
    g = _proj(h_p, h_s, w, 2 * qk_width + v_width, v_width, BF16, F32, name="ret_g_proj")
    return q, k, v, g


def _residual_proj(a_p, a_s, w, x_p, x_s, name):
    return _proj(a_p, a_s, w, 0, w.shape[1], F32, F32, _epi_residual, [("tile", x_p, x_s)],
                 name=name)


def kernel(x_prompt, x_sample, cache_k, cache_v, state_ret, page_table, attn_norm, w_in_attn,
           q_norm, k_norm, w_out_attn, ret_norm, w_in_ret, ret_head_norm, w_out_ret):
    batch, seq, d = x_prompt.shape
    db, dec_seq, _ = x_sample.shape
    assert dec_seq == 1
    n_attn, n_ret = attn_norm.shape[0], ret_norm.shape[0]
    hd = q_norm.shape[-1]
    width = w_out_attn.shape[1]
    heads = width // hd
    dv = ret_head_norm.shape[-1]
    v_width = w_out_ret.shape[1]
    ret_heads = v_width // dv
    qk_width = (w_in_ret.shape[-1] - 2 * v_width) // 2
    dk = qk_width // ret_heads
    assert dk == 2 * LANES
    n_pages = page_table.shape[1]
    past_len = n_pages * cache_k.shape[2]
    assert cache_k.shape[2] == PAGE_SIZE and past_len % MOBA_BLOCK == 0
    assert cache_k.shape[3:] == (heads, hd)

    m = batch * seq
    sp = BF16_SUBLANES * (-(-db // BF16_SUBLANES))
    xp = x_prompt.reshape(m, d)
    xs = jnp.pad(x_sample.reshape(db, d), ((0, sp - db), (0, 0)))
    pos_p = jnp.arange(seq, dtype=jnp.int32)
    pos_s = jnp.full((sp,), past_len, jnp.int32)

    def pad_rows(t):
        return jnp.pad(t.reshape(db, -1), ((0, sp - db), (0, 0)))

    kp_l, vp_l, ks_l, vs_l, sp_l, ss_l = [], [], [], [], [], []
    for layer in range(n_attn + n_ret):
        j = layer // 2
        if layer % 2 == 0:
            h_p, h_s = _rmsnorm_bf16(xp, attn_norm[j]), _rmsnorm_bf16(xs, attn_norm[j])
            (q_p, q_s), (k_p, k_s), (v_p, v_s), (g_p, g_s) = _moba_in_proj(
                h_p, h_s, w_in_attn[j], q_norm[j], k_norm[j], pos_p, pos_s, width, hd)
            o_p, kmean = _moba_prompt(q_p, k_p, v_p, g_p, batch, seq, hd, cache_k, j, page_table)
            q_s, k_s, v_s, g_s = (t[:db].reshape(db, 1, width) for t in (q_s, k_s, v_s, g_s))
            if kmean is None:
                kmean = _paged_block_means(cache_k, j, page_table)
            kmean = kmean.reshape(db, kmean.shape[1], width)
            sel = _gate_topk(q_s, kmean, heads, hd)[:, :MOBA_TOPK, :heads]
            sel = jnp.swapaxes(sel, 1, 2).reshape(-1)
            o_s = _moba_decode(q_s, k_s, v_s, g_s, cache_k, cache_v, j, sel, page_table)
            xp, xs = _residual_proj(o_p, pad_rows(o_s), w_out_attn[j], xp, xs, "attn_out_proj")
            kp_l.append(k_p.reshape(batch, seq, heads, hd))
            vp_l.append(v_p.reshape(batch, seq, heads, hd))
            ks_l.append(k_s.reshape(db, 1, heads, hd))
            vs_l.append(v_s.reshape(db, 1, heads, hd))
        else:
            h_p, h_s = _rmsnorm_bf16(xp, ret_norm[j]), _rmsnorm_bf16(xs, ret_norm[j])
            (q_p, q_s), (k_p, k_s), (v_p, v_s), (g_p, g_s) = _ret_in_proj(
                h_p, h_s, w_in_ret[j], pos_p, pos_s, qk_width, v_width, dk)
            o_p, s_p = _ret_prompt(q_p, k_p, v_p, g_p, ret_head_norm[j], batch, seq, ret_heads)
            q_s, k_s, v_s, g_s = (t[:db].reshape(db, 1, -1) for t in (q_s, k_s, v_s, g_s))
            o_s, s_s = _ret_decode(q_s, k_s, v_s, g_s, state_ret[j], ret_head_norm[j], ret_heads)
            xp, xs = _residual_proj(o_p, pad_rows(o_s), w_out_ret[j], xp, xs, "ret_out_proj")
            sp_l.append(s_p)
            ss_l.append(s_s)

    return (xp.reshape(batch, seq, d), xs[:db].reshape(db, 1, d),
            jnp.stack(kp_l), jnp.stack(vp_l), jnp.stack(ks_l), jnp.stack(vs_l),
            jnp.stack(sp_l), jnp.stack(ss_l))
```

```python
import functools
import math

import jax
import jax.numpy as jnp
from jax import lax
from jax.experimental import pallas as pl
from jax.experimental.pallas import tpu as pltpu

EPS = 1e-6
PAGE_SIZE = 128
MOBA_BLOCK = 256
MOBA_TOPK = 3
ROPE_THETA = 10000.0
RET_CHUNK = 128

F32 = jnp.float32
BF16 = jnp.bfloat16
NEG_INF = float("-inf")

LANES = 128
SUBLANES = 8
BF16_SUBLANES = 16
V7X_VMEM_BYTES = 64 * 1024 * 1024
VMEM_LIMIT_BYTES = V7X_VMEM_BYTES * 7 // 8

PROJ_TM = 1024
PROJ_TN = 512
PROJ_TK = 4096
PROJ_DMA_PARTS = 4
PROJ_SUBTILES = 2
KMEAN_BLOCKS_PER_STEP = 2
RET_DECODE_HEADS_PER_STEP = 4

_NT = (((1,), (1,)), ((), ()))
_TN = (((0,), (0,)), ((), ()))


def _params(*sem):
    return pltpu.CompilerParams(dimension_semantics=sem, vmem_limit_bytes=VMEM_LIMIT_BYTES)


def _tile(n, target):
    if n <= target:
        return n
    t = target
    while n % t:
        t -= LANES if t > LANES else 1
    return t


def _silu(g):
    return g / (1.0 + jnp.exp(-g))


def _rmsnorm_kernel(x_ref, g_ref, o_ref):
    x = x_ref[...]
    ms = jnp.mean(x * x, axis=-1, keepdims=True)
    o_ref[...] = (x * lax.rsqrt(ms + EPS) * g_ref[...]).astype(o_ref.dtype)


def _rmsnorm_bf16(x, g):
    m, d = x.shape
    tm = _tile(m, 256)
    return pl.pallas_call(
        _rmsnorm_kernel,
        grid=(m // tm,),
        in_specs=[pl.BlockSpec((tm, d), lambda i: (i, 0)),
                  pl.BlockSpec((1, d), lambda i: (0, 0))],
        out_specs=pl.BlockSpec((tm, d), lambda i: (i, 0)),
        out_shape=jax.ShapeDtypeStruct((m, d), BF16),
        compiler_params=_params("parallel"),
        name="rmsnorm",
    )(x, g.reshape(1, d))


def _epi_plain(acc, rows):
    return acc


def _epi_residual(acc, rows, x_ref):
    return x_ref[rows, :] + acc


def _epi_headnorm_rope(acc, rows, gn_ref, cos_ref, sin_ref):
    hd = gn_ref.shape[-1]
    gn, cos2, sin2 = gn_ref[...], cos_ref[rows, :], sin_ref[rows, :]
    outs = []
    for c in range(acc.shape[1] // hd):
        z = acc[:, c * hd:(c + 1) * hd]
        ms = jnp.mean(z * z, axis=-1, keepdims=True)
        zn = z * lax.rsqrt(ms + EPS) * gn
        outs.append(zn * cos2 + pltpu.roll(zn, hd // 2, 1) * sin2)
    return jnp.concatenate(outs, axis=1)


def _epi_rope_wide(acc, rows, cos_ref, sin_ref, *, scale):
    half = cos_ref.shape[-1]
    cos, sin = cos_ref[rows, :], sin_ref[rows, :]
    outs = []
    for c in range(acc.shape[1] // (2 * half)):
        x1 = acc[:, (2 * c) * half:(2 * c + 1) * half]
        x2 = acc[:, (2 * c + 1) * half:(2 * c + 2) * half]
        outs.append((x1 * cos - x2 * sin) * scale)
        outs.append((x2 * cos + x1 * sin) * scale)
    return jnp.concatenate(outs, axis=1)


def _proj_kernel(*refs, epilogue, n_extra, nk, n_parts):
    ap_refs, as_ref, w_refs = refs[:n_parts], refs[n_parts], refs[n_parts + 1:2 * n_parts + 1]
    refs = refs[2 * n_parts + 1:]
    extras_p = refs[:n_extra]
    extras_s = refs[n_extra:2 * n_extra]
    op_ref, os_ref, wbf_ref = refs[2 * n_extra:2 * n_extra + 3]
    acc_refs = refs[2 * n_extra + 3:]
    i, t = pl.program_id(1), pl.program_id(2)

    def tile(a_ref, rows, acc_ref, o_ref, extras):
        acc = jnp.dot(a_ref[...], wbf_ref[t], preferred_element_type=F32)
        if nk == 1:
            o_ref[rows, :] = epilogue(acc, rows, *extras).astype(o_ref.dtype)
            return

        @pl.when(t == 0)
        def _():
            acc_ref[rows, :] = acc

        if nk > 2:
            @pl.when(jnp.logical_and(t > 0, t < nk - 1))
            def _():
                acc_ref[rows, :] += acc

        @pl.when(t == nk - 1)
        def _():
            o_ref[rows, :] = epilogue(acc_ref[rows, :] + acc, rows, *extras).astype(o_ref.dtype)

    @pl.when(i == 0)
    def _():
        rows_k = w_refs[0].shape[0]
        for c, w_ref in enumerate(w_refs):
            wbf_ref[t, c * rows_k:(c + 1) * rows_k, :] = w_ref[...].astype(BF16)
        tile(as_ref, slice(None), acc_refs[1] if nk > 1 else None, os_ref, extras_s)

    rows_m = ap_refs[0].shape[0]
    for c, a_ref in enumerate(ap_refs):
        tile(a_ref, slice(c * rows_m, (c + 1) * rows_m), acc_refs[0] if nk > 1 else None, op_ref,
             extras_p)


def _proj(a_p, a_s, w, col0, n_out, dtype_p, dtype_s, epilogue=_epi_plain, extras=(), name="proj"):
    m, k = a_p.shape
    s = a_s.shape[0]
    tm, tn, tk = _tile(m, PROJ_TM), _tile(n_out, PROJ_TN), _tile(k, PROJ_TK)
    nk = k // tk
    n_tiles = n_out // tn
    n_sub = PROJ_SUBTILES if nk == 1 and n_tiles % PROJ_SUBTILES == 0 else 1
    steps = n_sub if nk == 1 else nk
    n_groups = n_tiles // n_sub
    split = nk == 1 and tm % PROJ_DMA_PARTS == 0 and tk % PROJ_DMA_PARTS == 0
    n_parts = PROJ_DMA_PARTS if split else 1
    rows_m, rows_k = tm // n_parts, tk // n_parts
    assert col0 % tn == 0
    off = col0 // tn

    def col(g, i, t):
        return g * n_sub + t if nk == 1 else g

    def col_first_rows_only(g, i, t):
        return jnp.where(i == 0, col(g, i, t), g * n_sub + n_sub - 1) if nk == 1 else g

    def kchunk(t):
        return 0 if nk == 1 else t

    def extra_specs(kind, arr, rows_per_tile, is_prompt):
        if kind == "const":
            return pl.BlockSpec(arr.shape, lambda g, i, t: (0,) * arr.ndim)
        if kind == "rows":
            assert arr.shape[0] % rows_per_tile == 0
            period = arr.shape[0] // rows_per_tile
            if is_prompt:
                return pl.BlockSpec((rows_per_tile, arr.shape[1]), lambda g, i, t: (i % period, 0))
            return pl.BlockSpec((rows_per_tile, arr.shape[1]), lambda g, i, t: (0, 0))
        assert kind == "tile"
        if is_prompt:
            return pl.BlockSpec((rows_per_tile, tn), lambda g, i, t: (i, col(g, i, t)))
        return pl.BlockSpec((rows_per_tile, tn), lambda g, i, t: (0, col_first_rows_only(g, i, t)))

    def a_spec(c):
        return pl.BlockSpec((rows_m, tk), lambda g, i, t: (i * n_parts + c, kchunk(t)))

    def w_index(g, i, t):
        more = g + 1 < n_groups
        if nk == 1:
            ahead = jnp.where(more, (g + 1) * n_sub, g * n_sub + n_sub - 1)
            return 0, jnp.where(i == 0, g * n_sub + t, ahead)
        return (jnp.where(i == 0, t, jnp.where(more, 0, nk - 1)),
                jnp.where(i == 0, g, jnp.minimum(g + 1, n_groups - 1)))

    def w_spec(c):
        def imap(g, i, t):
            kc, tile = w_index(g, i, t)
            return kc * n_parts + c, off + tile
        return pl.BlockSpec((rows_k, tn), imap)

    in_specs = ([a_spec(c) for c in range(n_parts)]
                + [pl.BlockSpec((s, tk), lambda g, i, t: (0, kchunk(t)))]
                + [w_spec(c) for c in range(n_parts)])
    in_specs += [extra_specs(kind, ap, tm, True) for kind, ap, _ in extras]
    in_specs += [extra_specs(kind, asm, s, False) for kind, _, asm in extras]
    operands = ([a_p] * n_parts + [a_s] + [w] * n_parts + [ap for _, ap, _ in extras]
                + [asm for _, _, asm in extras])
    scratch = [pltpu.VMEM((steps, tk, tn), BF16)]
    if nk > 1:
        scratch += [pltpu.VMEM((tm, tn), F32), pltpu.VMEM((s, tn), F32)]
    return pl.pallas_call(
        functools.partial(_proj_kernel, epilogue=epilogue, n_extra=len(extras), nk=nk,
                          n_parts=n_parts),
        grid=(n_groups, m // tm, steps),
        in_specs=in_specs,
        out_specs=[pl.BlockSpec((tm, tn), lambda g, i, t: (i, col(g, i, t))),
                   pl.BlockSpec((s, tn), lambda g, i, t: (0, col_first_rows_only(g, i, t)))],
        out_shape=[jax.ShapeDtypeStruct((m, n_out), dtype_p),
                   jax.ShapeDtypeStruct((s, n_out), dtype_s)],
        scratch_shapes=scratch,
        compiler_params=_params("parallel", "arbitrary", "arbitrary"),
        name=name,
    )(*operands)


def _block_means(page_refs, o_ref, ppb):
    for blk in range(len(page_refs) // ppb):
        tot = jnp.sum(page_refs[blk * ppb][...], axis=0)
        for r in page_refs[blk * ppb + 1:(blk + 1) * ppb]:
            tot = tot + jnp.sum(r[...], axis=0)
        o_ref[blk] = tot * (1.0 / MOBA_BLOCK)


def _moba_prompt_kernel(*refs, scale, n_pages_in, ppb):
    if n_pages_in:
        refs = refs[1:]
    q_ref, k_ref, v_ref, g_ref = refs[:4]
    page_refs = refs[4:4 + n_pages_in]
    o_ref = refs[4 + n_pages_in]
    k16_ref, vt16_ref = refs[-2:]
    if n_pages_in:
        _block_means(page_refs, refs[5 + n_pages_in], ppb)
    seq, hd = q_ref.shape
    blk, topk = MOBA_BLOCK, MOBA_TOPK
    nb = seq // blk
    k16_ref[...] = k_ref[...].astype(BF16)
    vt16_ref[...] = v_ref[...].T.astype(BF16)
    km_rows = SUBLANES * max(1, -(-(nb - 1) // SUBLANES))
    row_id = lax.broadcasted_iota(jnp.int32, (km_rows, hd), 0)
    kmean = jnp.zeros((km_rows, hd), F32)
    for n in range(nb - 1):
        km = jnp.mean(k_ref[n * blk:(n + 1) * blk, :], axis=0, keepdims=True)
        kmean = jnp.where(row_id == n, km, kmean)
    kmean16 = kmean.astype(BF16)
    key_id = lax.broadcasted_iota(jnp.int32, (blk, blk), 0)
    qry_id = lax.broadcasted_iota(jnp.int32, (blk, blk), 1)
    causal_bias = jnp.where(key_id <= qry_id, 0.0, NEG_INF).astype(F32)
    c = scale * math.log2(math.e)

    def scores(j):
        qj = q_ref[j * blk:(j + 1) * blk, :]
        st = lax.dot_general(k16_ref[0:(j + 1) * blk, :], qj, _NT,
                             preferred_element_type=F32)
        gt = None
        if j > topk:
            gt = lax.dot_general(kmean16, qj, _NT, preferred_element_type=F32)
        return st, gt

    def softmax(j, st, gt):
        biases = None
        if j > topk:
            gates = [gt[n:n + 1, :] for n in range(j)]
            ranks = [jnp.zeros((1, blk), F32) for _ in range(j)]
            for lo in range(j):
                for hi in range(lo + 1, j):
                    lo_ahead = jnp.where(gates[lo] >= gates[hi], 1.0, 0.0)
                    ranks[hi] = ranks[hi] + lo_ahead
                    ranks[lo] = ranks[lo] + (1.0 - lo_ahead)
            biases = [jnp.where(r < topk, 0.0, NEG_INF).astype(F32) for r in ranks]
        pieces = []
        for n in range(j + 1):
            sn = st[n * blk:(n + 1) * blk, :]
            if n == j:
                sn = sn + causal_bias
            elif biases is not None:
                sn = sn + biases[n]
            pieces.append(sn)
        mx = jnp.max(pieces[0], axis=0, keepdims=True)
        for sn in pieces[1:]:
            mx = jnp.maximum(mx, jnp.max(sn, axis=0, keepdims=True))
        probs = [jnp.exp2((sn - mx) * c) for sn in pieces]
        denom = jnp.sum(probs[0], axis=0, keepdims=True)
        for p in probs[1:]:
            denom = denom + jnp.sum(p, axis=0, keepdims=True)
        return jnp.concatenate(probs, axis=0).astype(BF16), denom

    def finish(j, pt16, denom):
        rows = slice(j * blk, (j + 1) * blk)
        ot = jnp.dot(vt16_ref[:, 0:(j + 1) * blk], pt16, preferred_element_type=F32) / denom
        o_ref[rows, :] = (ot.T * _silu(g_ref[rows, :].astype(F32))).astype(o_ref.dtype)

    nxt = scores(0)
    pending = None
    for j in range(nb):
        cur = nxt
        if j + 1 < nb:
            nxt = scores(j + 1)
        if pending is not None:
            finish(*pending)
        pending = (j,) + softmax(j, *cur)
    finish(*pending)


def _moba_prompt(q, k, v, g, batch, seq, hd, cache, layer, page_table):
    m, width = q.shape
    heads = width // hd
    assert seq % MOBA_BLOCK == 0 and hd == LANES
    _, _, page, c_heads, c_hd = cache.shape
    ppb = MOBA_BLOCK // page
    total_pages = page_table.size
    steps = batch * heads
    pps = total_pages // steps
    page_bytes = page * c_heads * c_hd * 4
    own_bytes = seq * hd * (2 * (3 * 2 + 2 * 4) + 2 * 2) + 2 * seq * MOBA_BLOCK * 4
    hosted = (total_pages % steps == 0 and pps % ppb == 0 and page_table.shape[1] % ppb == 0
              and own_bytes + 2 * pps * page_bytes <= VMEM_LIMIT_BYTES)
    if not hosted:
        pps = 0
    bps = pps // ppb
    spec = pl.BlockSpec((seq, hd), lambda b, h, *_: (b, h))

    def page_spec(r):
        return pl.BlockSpec((None, None, page, c_heads, c_hd),
                            lambda b, h, pt: (layer, pt[(b * heads + h) * pps + r], 0, 0, 0))

    out_specs, out_shape = [spec], [jax.ShapeDtypeStruct((m, width), BF16)]
    if hosted:
        out_specs.append(pl.BlockSpec((bps, c_heads, c_hd), lambda b, h, pt: (b * heads + h, 0, 0)))
        out_shape.append(jax.ShapeDtypeStruct((total_pages // ppb, c_heads, c_hd), F32))
    outs = pl.pallas_call(
        functools.partial(_moba_prompt_kernel, scale=hd ** -0.5, n_pages_in=pps, ppb=ppb),
        grid_spec=pltpu.PrefetchScalarGridSpec(
            num_scalar_prefetch=1 if hosted else 0,
            grid=(batch, heads),
            in_specs=[spec] * 4 + [page_spec(r) for r in range(pps)],
            out_specs=out_specs,
            scratch_shapes=[pltpu.VMEM((seq, hd), BF16), pltpu.VMEM((hd, seq), BF16)],
        ),
        out_shape=out_shape,
        compiler_params=_params("parallel", "parallel"),
        name="moba_prompt",
    )(*([page_table.reshape(-1)] if hosted else []), q, k, v, g, *([cache] * pps))
    if hosted:
        return outs[0], outs[1].reshape(page_table.shape[0], -1, c_heads, c_hd)
    return outs[0], None


def _kmean_kernel(pt_ref, *refs, ppb):
    *page_refs, o_ref = refs
    _block_means(page_refs, o_ref, ppb)


def _paged_block_means(cache, layer, page_table):
    _, _, page, heads, hd = cache.shape
    db, n_pages = page_table.shape
    ppb = MOBA_BLOCK // page
    assert n_pages % ppb == 0
    nblk = n_pages // ppb
    bps = _tile(nblk, KMEAN_BLOCKS_PER_STEP)
    pps = bps * ppb

    def page_spec(r):
        return pl.BlockSpec((None, None, page, heads, hd),
                            lambda b, n, pt: (layer, pt[b * n_pages + n * pps + r], 0, 0, 0))

    return pl.pallas_call(
        functools.partial(_kmean_kernel, ppb=ppb),
        grid_spec=pltpu.PrefetchScalarGridSpec(
            num_scalar_prefetch=1,
            grid=(db, nblk // bps),
            in_specs=[page_spec(r) for r in range(pps)],
            out_specs=pl.BlockSpec((None, bps, heads, hd), lambda b, n, pt: (b, n, 0, 0)),
        ),
        out_shape=jax.ShapeDtypeStruct((db, nblk, heads, hd), F32),
        compiler_params=_params("parallel", "arbitrary"),
        name="paged_block_means",
    )(page_table.reshape(-1), *([cache] * pps))


def _gate_topk_kernel(q_ref, km_ref, o_ref, *, heads, hd):
    q = q_ref[...].astype(BF16).astype(F32)
    km = km_ref[...].astype(BF16).astype(F32)
    prod = km * q
    nblk = km.shape[0]
    lane = lax.broadcasted_iota(jnp.int32, (nblk, LANES), 1)
    gate = jnp.full((nblk, LANES), NEG_INF, F32)
    for h in range(heads):
        gh = jnp.sum(prod[:, h * hd:(h + 1) * hd], axis=1, keepdims=True)
        gate = jnp.where(lane == h, gh, gate)
    rowf = lax.broadcasted_iota(jnp.int32, (nblk, LANES), 0).astype(F32)
    out_row = lax.broadcasted_iota(jnp.int32, o_ref.shape, 0)
    out = jnp.zeros(o_ref.shape, F32)
    for t in range(MOBA_TOPK):
        mx = jnp.max(gate, axis=0, keepdims=True)
        idx = jnp.min(jnp.where(gate == mx, rowf, float(nblk)), axis=0, keepdims=True)
        out = jnp.where(out_row == t, idx, out)
        gate = jnp.where(rowf == idx, NEG_INF, gate)
    o_ref[...] = out.astype(jnp.int32)


def _gate_topk(q, kmean, heads, hd):
    db, nblk, width = kmean.shape
    assert heads <= LANES and nblk >= MOBA_TOPK
    return pl.pallas_call(
        functools.partial(_gate_topk_kernel, heads=heads, hd=hd),
        grid=(db,),
        in_specs=[pl.BlockSpec((None, 1, width), lambda b: (b, 0, 0)),
                  pl.BlockSpec((None, nblk, width), lambda b: (b, 0, 0))],
        out_specs=pl.BlockSpec((None, SUBLANES, LANES), lambda b: (b, 0, 0)),
        out_shape=jax.ShapeDtypeStruct((db, SUBLANES, LANES), jnp.int32),
        compiler_params=_params("parallel"),
        name="gate_topk",
    )(q, kmean)


def _moba_decode_kernel(sel_ref, pt_ref, q_ref, kn_ref, vn_ref, g_ref, ck_hbm, cv_hbm, o_ref,
                        kbuf, vbuf, sem, *, scale, layer, n_pages, ppb):
    b, h = pl.program_id(0), pl.program_id(1)
    heads = pl.num_programs(1)
    step = b * heads + h
    n_steps = pl.num_programs(0) * heads
    slot = step % 2
    n = kbuf.shape[1]

    def copies(bb, hh, sl):
        out = []
        for t in range(MOBA_TOPK):
            blk = sel_ref[(bb * heads + hh) * MOBA_TOPK + t]
            for r in range(ppb):
                page = pt_ref[bb * n_pages + blk * ppb + r]
                i = t * ppb + r
                out.append(pltpu.make_async_copy(ck_hbm.at[layer, page, :, hh, :],
                                                 kbuf.at[sl, i], sem.at[sl]))
                out.append(pltpu.make_async_copy(cv_hbm.at[layer, page, :, hh, :],
                                                 vbuf.at[sl, i], sem.at[sl]))
        return out

    @pl.when(step == 0)
    def _():
        for cp in copies(b, h, slot):
            cp.start()

    @pl.when(step + 1 < n_steps)
    def _():
        nxt = step + 1
        for cp in copies(nxt // heads, nxt % heads, 1 - slot):
            cp.start()

    for cp in copies(b, h, slot):
        cp.wait()

    hd = q_ref.shape[-1]
    q = q_ref[...]
    q16 = jnp.broadcast_to(q, (BF16_SUBLANES, hd)).astype(BF16)
    scores = [lax.dot_general(q16, kbuf[slot, i].astype(BF16), _NT,
                              preferred_element_type=F32)[0:1, :] * scale for i in range(n)]
    qb = q.astype(BF16).astype(F32)
    s_new = jnp.sum(qb * kn_ref[...].astype(BF16).astype(F32), axis=1, keepdims=True) * scale
    mx = s_new
    for s in scores:
        mx = jnp.maximum(mx, jnp.max(s, axis=1, keepdims=True))
    p_new = jnp.exp(s_new - mx)
    denom = p_new
    acc = p_new.astype(BF16).astype(F32) * vn_ref[...].astype(BF16).astype(F32)
    for i, s in enumerate(scores):
        p = jnp.exp(s - mx)
        denom = denom + jnp.sum(p, axis=1, keepdims=True)
        p16 = jnp.broadcast_to(p, (BF16_SUBLANES, p.shape[1])).astype(BF16)
        acc = acc + jnp.dot(p16, vbuf[slot, i].astype(BF16), preferred_element_type=F32)[0:1, :]
    o = acc / denom
    o_ref[...] = (o * _silu(g_ref[...])).astype(o_ref.dtype)


def _moba_decode(q, k_new, v_new, g, cache_k, cache_v, layer, sel, page_table):
    db, _, width = q.shape
    _, _, page, heads, hd = cache_k.shape
    n_pages = page_table.shape[1]
    ppb = MOBA_BLOCK // page
    n = MOBA_TOPK * ppb
    row_spec = pl.BlockSpec((None, 1, hd), lambda b, h, sel, pt: (b, 0, h))
    hbm_spec = pl.BlockSpec(memory_space=pl.ANY)
    return pl.pallas_call(
        functools.partial(_moba_decode_kernel, scale=hd ** -0.5, layer=layer, n_pages=n_pages,
                          ppb=ppb),
        grid_spec=pltpu.PrefetchScalarGridSpec(
            num_scalar_prefetch=2,
            grid=(db, heads),
            in_specs=[row_spec] * 4 + [hbm_spec, hbm_spec],
            out_specs=row_spec,
            scratch_shapes=[pltpu.VMEM((2, n, page, hd), F32), pltpu.VMEM((2, n, page, hd), F32),
                            pltpu.SemaphoreType.DMA((2,))],
        ),
        out_shape=jax.ShapeDtypeStruct((db, 1, width), BF16),
        compiler_params=_params("arbitrary", "arbitrary"),
        name="moba_decode",
    )(sel, page_table.reshape(-1), q, k_new, v_new, g, cache_k, cache_v)


def _head_norm_gate(o, hg, g):
    ms = jnp.mean(o * o, axis=-1, keepdims=True)
    return (o * lax.rsqrt(ms + EPS) * hg) * _silu(g)


def _ret_prompt_kernel(q_ref, k_ref, v_ref, g_ref, dm_ref, qd_ref, kd_ref, cd_ref, hg_ref,
                       o_ref, s_ref):
    c = dm_ref.shape[-1]
    n_chunks = q_ref.shape[0] // c
    s_ref[...] = jnp.zeros(s_ref.shape, F32)
    dm, qdec, kdec, cdec, hg = dm_ref[...], qd_ref[...], kd_ref[...], cd_ref[...], hg_ref[...]

    def local(ci):
        r = slice(ci * c, (ci + 1) * c)
        qc, kc, vc = q_ref[r, :], k_ref[r, :], v_ref[r, :]
        att = lax.dot_general(qc, kc, _NT, preferred_element_type=F32) * dm
        inner = jnp.dot(att.astype(BF16), vc, preferred_element_type=F32)
        qd = (qc.astype(F32) * qdec).astype(BF16)
        kd = (kc.astype(F32) * kdec).astype(BF16)
        return inner, qd, lax.dot_general(kd, vc, _TN, preferred_element_type=F32)

    def carry(ci, inner, qd, kv):
        r = slice(ci * c, (ci + 1) * c)
        s = s_ref[...]
        o = inner + jnp.dot(qd, s.astype(BF16), preferred_element_type=F32)
        s_ref[...] = s * cdec + kv
        o_ref[r, :] = _head_norm_gate(o, hg, g_ref[r, :].astype(F32)).astype(o_ref.dtype)

    nxt = local(0)
    for ci in range(n_chunks):
        cur = nxt
        if ci + 1 < n_chunks:
            nxt = local(ci + 1)
        carry(ci, *cur)


def _ret_tables(heads, c, dk, dv):
    log_g = jnp.log1p(-jnp.exp2(-5.0 - jnp.arange(heads, dtype=F32)))
    i = jnp.arange(c, dtype=F32)
    diff = i[:, None] - i[None, :]
    dmask = jnp.where(diff >= 0, jnp.exp(log_g[:, None, None] * jnp.maximum(diff, 0.0)), 0.0)
    q_dec = jnp.exp(log_g[:, None] * (i[None, :] + 1.0))
    k_dec = jnp.exp(log_g[:, None] * (c - 1.0 - i[None, :]))
    chunk_dec = jnp.exp(log_g * c)
    return (dmask,
            jnp.broadcast_to(q_dec[:, :, None], (heads, c, dk)),
            jnp.broadcast_to(k_dec[:, :, None], (heads, c, dk)),
            jnp.broadcast_to(chunk_dec[:, None, None], (heads, 1, dv)))


def _ret_prompt(q, k, v, g, head_g, batch, seq, heads):
    m = q.shape[0]
    dk, dv = q.shape[1] // heads, v.shape[1] // heads
    c = RET_CHUNK
    assert seq % c == 0
    dmask, qdec, kdec, cdec = _ret_tables(heads, c, dk, dv)
    qk_spec = pl.BlockSpec((seq, dk), lambda b, h: (b, h))
    v_spec = pl.BlockSpec((seq, dv), lambda b, h: (b, h))

    def head_spec(*shape):
        return pl.BlockSpec((None,) + shape, lambda b, h: (h, 0, 0))

    return pl.pallas_call(
        _ret_prompt_kernel,
        grid=(batch, heads),
        in_specs=[qk_spec, qk_spec, v_spec, v_spec, head_spec(c, c), head_spec(c, dk),
                  head_spec(c, dk), head_spec(1, dv), pl.BlockSpec((1, dv), lambda b, h: (0, 0))],
        out_specs=[v_spec, pl.BlockSpec((None, None, dk, dv), lambda b, h: (b, h, 0, 0))],
        out_shape=[jax.ShapeDtypeStruct((m, heads * dv), BF16),
                   jax.ShapeDtypeStruct((batch, heads, dk, dv), F32)],
        compiler_params=_params("parallel", "parallel"),
        name="ret_prompt",
    )(q, k, v, g, dmask, qdec, kdec, cdec, head_g.reshape(1, dv))


def _ret_decode_kernel(q_ref, k_ref, kcol_ref, v_ref, g_ref, s0_ref, gam_ref, hg_ref,
                       o_ref, s_ref):
    group, dk, dv = s0_ref.shape
    for c in range(group):
        qk = slice(c * dk, (c + 1) * dk)
        vv = slice(c * dv, (c + 1) * dv)
        q, k, v, s0, gam = q_ref[:, qk], k_ref[:, qk], v_ref[:, vv], s0_ref[c], gam_ref[c]
        att = jnp.sum(q.astype(BF16).astype(F32) * k.astype(BF16).astype(F32), axis=1,
                      keepdims=True)
        qd = jnp.broadcast_to(q * gam[:, :dk], (BF16_SUBLANES, dk)).astype(BF16)
        o = att * v + jnp.dot(qd, s0.astype(BF16), preferred_element_type=F32)[0:1, :]
        s_ref[c] = s0 * gam + kcol_ref[c] * v
        o_ref[:, vv] = _head_norm_gate(o, hg_ref[...], g_ref[:, vv]).astype(o_ref.dtype)


def _ret_decode(q, k, v, g, s0, head_g, heads):
    db = q.shape[0]
    dk, dv = q.shape[2] // heads, v.shape[2] // heads
    assert dv >= dk
    group = _tile(heads, RET_DECODE_HEADS_PER_STEP)
    gamma = jnp.exp(jnp.log1p(-jnp.exp2(-5.0 - jnp.arange(heads, dtype=F32))) * 1.0)
    gam = jnp.broadcast_to(gamma[:, None, None], (heads, 1, dv))
    kcol = k.reshape(db, heads, dk, 1)
    qk_spec = pl.BlockSpec((None, 1, group * dk), lambda b, h: (b, 0, h))
    v_spec = pl.BlockSpec((None, 1, group * dv), lambda b, h: (b, 0, h))
    s_spec = pl.BlockSpec((None, group, dk, dv), lambda b, h: (b, h, 0, 0))
    return pl.pallas_call(
        _ret_decode_kernel,
        grid=(db, heads // group),
        in_specs=[qk_spec, qk_spec,
                  pl.BlockSpec((None, group, dk, 1), lambda b, h: (b, h, 0, 0)),
                  v_spec, v_spec, s_spec,
                  pl.BlockSpec((group, 1, dv), lambda b, h: (h, 0, 0)),
                  pl.BlockSpec((1, dv), lambda b, h: (0, 0))],
        out_specs=[v_spec, s_spec],
        out_shape=[jax.ShapeDtypeStruct((db, 1, heads * dv), BF16),
                   jax.ShapeDtypeStruct(s0.shape, F32)],
        compiler_params=_params("parallel", "parallel"),
        name="ret_decode",
    )(q, k, kcol, v, g, s0, gam, head_g.reshape(1, dv))


def _rope_tables(pos, dim):
    half = dim // 2
    inv = ROPE_THETA ** (-jnp.arange(half, dtype=F32) / half)
    ang = pos.astype(F32)[:, None] * inv[None, :]
    return jnp.cos(ang), jnp.sin(ang)


def _moba_in_proj(h_p, h_s, w, qn, kn, pos_p, pos_s, width, hd):
    def tables(pos):
        cos, sin = _rope_tables(pos, hd)
        return jnp.concatenate([cos, cos], axis=-1), jnp.concatenate([-sin, sin], axis=-1)

    (cos_p, sin_p), (cos_s, sin_s) = tables(pos_p), tables(pos_s)
    rope = [("rows", cos_p, cos_s), ("rows", sin_p, sin_s)]
    qn, kn = qn.reshape(1, hd), kn.reshape(1, hd)
    q = _proj(h_p, h_s, w, 0, width, BF16, F32, _epi_headnorm_rope,
              [("const", qn, qn)] + rope, name="attn_q_proj")
    k = _proj(h_p, h_s, w, width, width, F32, F32, _epi_headnorm_rope,
              [("const", kn, kn)] + rope, name="attn_k_proj")
    v = _proj(h_p, h_s, w, 2 * width, width, F32, F32, name="attn_v_proj")
    g = _proj(h_p, h_s, w, 3 * width, width, BF16, F32, name="attn_g_proj")
    return q, k, v, g


def _ret_in_proj(h_p, h_s, w, pos_p, pos_s, qk_width, v_width, dk):
    (cos_p, sin_p), (cos_s, sin_s) = _rope_tables(pos_p, dk), _rope_tables(pos_s, dk)
    rope = [("rows", cos_p, cos_s), ("rows", sin_p, sin_s)]
    q = _proj(h_p, h_s, w, 0, qk_width, BF16, F32,
              functools.partial(_epi_rope_wide, scale=1.0), rope, name="ret_q_proj")
    k = _proj(h_p, h_s, w, qk_width, qk_width, BF16, F32,
              functools.partial(_epi_rope_wide, scale=dk ** -0.5), rope, name="ret_k_proj")
    v = _proj(h_p, h_s, w, 2 * qk_width, v_width, BF16, F32, name="ret_v_proj")
    g = _proj(h_p, h_s, w, 2 * qk_width + v_width, v_width, BF16, F32, name="ret_g_proj")
    return q, k, v, g


def _residual_proj(a_p, a_s, w, x_p, x_s, name):
    return _proj(a_p, a_s, w, 0, w.shape[1], F32, F32, _epi_residual, [("tile", x_p, x_s)],
                 name=name)


def kernel(x_prompt, x_sample, cache_k, cache_v, state_ret, page_table, attn_norm, w_in_attn,
           q_norm, k_norm, w_out_attn, ret_norm, w_in_ret, ret_head_norm, w_out_ret):
    batch, seq, d = x_prompt.shape
    db, dec_seq, _ = x_sample.shape
    assert dec_seq == 1
    n_attn, n_ret = attn_norm.shape[0], ret_norm.shape[0]
    hd = q_norm.shape[-1]
    width = w_out_attn.shape[1]
    heads = width // hd
    dv = ret_head_norm.shape[-1]
    v_width = w_out_ret.shape[1]
    ret_heads = v_width // dv
    qk_width = (w_in_ret.shape[-1] - 2 * v_width) // 2
    dk = qk_width // ret_heads
    assert dk == 2 * LANES
    n_pages = page_table.shape[1]
    past_len = n_pages * cache_k.shape[2]
    assert cache_k.shape[2] == PAGE_SIZE and past_len % MOBA_BLOCK == 0
    assert cache_k.shape[3:] == (heads, hd)

    m = batch * seq
    sp = BF16_SUBLANES * (-(-db // BF16_SUBLANES))
    xp = x_prompt.reshape(m, d)
    xs = jnp.pad(x_sample.reshape(db, d), ((0, sp - db), (0, 0)))
    pos_p = jnp.arange(seq, dtype=jnp.int32)
    pos_s = jnp.full((sp,), past_len, jnp.int32)

    def pad_rows(t):
        return jnp.pad(t.reshape(db, -1), ((0, sp - db), (0, 0)))

    kp_l, vp_l, ks_l, vs_l, sp_l, ss_l = [], [], [], [], [], []
    for layer in range(n_attn + n_ret):
        j = layer // 2
        if layer % 2 == 0:
            h_p, h_s = _rmsnorm_bf16(xp, attn_norm[j]), _rmsnorm_bf16(xs, attn_norm[j])
            (q_p, q_s), (k_p, k_s), (v_p, v_s), (g_p, g_s) = _moba_in_proj(
                h_p, h_s, w_in_attn[j], q_norm[j], k_norm[j], pos_p, pos_s, width, hd)
            o_p, kmean = _moba_prompt(q_p, k_p, v_p, g_p, batch, seq, hd, cache_k, j, page_table)
            q_s, k_s, v_s, g_s = (t[:db].reshape(db, 1, width) for t in (q_s, k_s, v_s, g_s))
            if kmean is None:
                kmean = _paged_block_means(cache_k, j, page_table)
            kmean = kmean.reshape(db, kmean.shape[1], width)
            sel = _gate_topk(q_s, kmean, heads, hd)[:, :MOBA_TOPK, :heads]
            sel = jnp.swapaxes(sel, 1, 2).reshape(-1)
            o_s = _moba_decode(q_s, k_s, v_s, g_s, cache_k, cache_v, j, sel, page_table)
            xp, xs = _residual_proj(o_p, pad_rows(o_s), w_out_attn[j], xp, xs, "attn_out_proj")
            kp_l.append(k_p.reshape(batch, seq, heads, hd))
            vp_l.append(v_p.reshape(batch, seq, heads, hd))
            ks_l.append(k_s.reshape(db, 1, heads, hd))
            vs_l.append(v_s.reshape(db, 1, heads, hd))
        else:
            h_p, h_s = _rmsnorm_bf16(xp, ret_norm[j]), _rmsnorm_bf16(xs, ret_norm[j])
            (q_p, q_s), (k_p, k_s), (v_p, v_s), (g_p, g_s) = _ret_in_proj(
                h_p, h_s, w_in_ret[j], pos_p, pos_s, qk_width, v_width, dk)
            o_p, s_p = _ret_prompt(q_p, k_p, v_p, g_p, ret_head_norm[j], batch, seq, ret_heads)
            q_s, k_s, v_s, g_s = (t[:db].reshape(db, 1, -1) for t in (q_s, k_s, v_s, g_s))
            o_s, s_s = _ret_decode(q_s, k_s, v_s, g_s, state_ret[j], ret_head_norm[j], ret_heads)
            xp, xs = _residual_proj(o_p, pad_rows(o_s), w_out_ret[j], xp, xs, "ret_out_proj")
            sp_l.append(s_p)
            ss_l.append(s_s)

    return (xp.reshape(batch, seq, d), xs[:db].reshape(db, 1, d),
            jnp.stack(kp_l), jnp.stack(vp_l), jnp.stack(ks_l), jnp.stack(vs_l),
            jnp.stack(sp_l), jnp.stack(ss_l))
```

```python
import functools
import math

import jax
import jax.numpy as jnp
from jax import lax
from jax.experimental import pallas as pl
from jax.experimental.pallas import tpu as pltpu

EPS = 1e-6
PAGE_SIZE = 128
MOBA_BLOCK = 256
MOBA_TOPK = 3
ROPE_THETA = 10000.0
RET_CHUNK = 128

F32 = jnp.float32
BF16 = jnp.bfloat16
NEG_INF = float("-inf")

LANES = 128
SUBLANES = 8
BF16_SUBLANES = 16
V7X_VMEM_BYTES = 64 * 1024 * 1024
VMEM_LIMIT_BYTES = V7X_VMEM_BYTES * 7 // 8

PROJ_TM = 1024
PROJ_TN = 512
PROJ_TK = 4096
PROJ_DMA_PARTS = 4
PROJ_SUBTILES = 2
KMEAN_BLOCKS_PER_STEP = 2
RET_DECODE_HEADS_PER_STEP = 4
MOBA_DECODE_HEADS_PER_STEP = 4

_NT = (((1,), (1,)), ((), ()))
_TN = (((0,), (0,)), ((), ()))


def _params(*sem):
    return pltpu.CompilerParams(dimension_semantics=sem, vmem_limit_bytes=VMEM_LIMIT_BYTES)


def _tile(n, target):
    if n <= target:
        return n
    t = target
    while n % t:
        t -= LANES if t > LANES else 1
    return t


def _silu(g):
    return g / (1.0 + jnp.exp(-g))


def _rmsnorm_kernel(x_ref, g_ref, o_ref):
    x = x_ref[...]
    ms = jnp.mean(x * x, axis=-1, keepdims=True)
    o_ref[...] = (x * lax.rsqrt(ms + EPS) * g_ref[...]).astype(o_ref.dtype)


def _rmsnorm_bf16(x, g):
    m, d = x.shape
    tm = _tile(m, 256)
    return pl.pallas_call(
        _rmsnorm_kernel,
        grid=(m // tm,),
        in_specs=[pl.BlockSpec((tm, d), lambda i: (i, 0)),
                  pl.BlockSpec((1, d), lambda i: (0, 0))],
        out_specs=pl.BlockSpec((tm, d), lambda i: (i, 0)),
        out_shape=jax.ShapeDtypeStruct((m, d), BF16),
        compiler_params=_params("parallel"),
        name="rmsnorm",
    )(x, g.reshape(1, d))


def _epi_plain(acc, rows):
    return acc


def _epi_residual(acc, rows, x_ref):
    return x_ref[rows, :] + acc


def _epi_headnorm_rope(acc, rows, gn_ref, cos_ref, sin_ref):
    hd = gn_ref.shape[-1]
    gn, cos2, sin2 = gn_ref[...], cos_ref[rows, :], sin_ref[rows, :]
    outs = []
    for c in range(acc.shape[1] // hd):
        z = acc[:, c * hd:(c + 1) * hd]
        ms = jnp.mean(z * z, axis=-1, keepdims=True)
        zn = z * lax.rsqrt(ms + EPS) * gn
        outs.append(zn * cos2 + pltpu.roll(zn, hd // 2, 1) * sin2)
    return jnp.concatenate(outs, axis=1)


def _epi_rope_wide(acc, rows, cos_ref, sin_ref, *, scale):
    half = cos_ref.shape[-1]
    cos, sin = cos_ref[rows, :], sin_ref[rows, :]
    outs = []
    for c in range(acc.shape[1] // (2 * half)):
        x1 = acc[:, (2 * c) * half:(2 * c + 1) * half]
        x2 = acc[:, (2 * c + 1) * half:(2 * c + 2) * half]
        outs.append((x1 * cos - x2 * sin) * scale)
        outs.append((x2 * cos + x1 * sin) * scale)
    return jnp.concatenate(outs, axis=1)


def _proj_kernel(*refs, epilogue, n_extra, nk, n_parts):
    ap_refs, as_ref, w_refs = refs[:n_parts], refs[n_parts], refs[n_parts + 1:2 * n_parts + 1]
    refs = refs[2 * n_parts + 1:]
    extras_p = refs[:n_extra]
    extras_s = refs[n_extra:2 * n_extra]
    op_ref, os_ref, wbf_ref = refs[2 * n_extra:2 * n_extra + 3]
    acc_refs = refs[2 * n_extra + 3:]
    i, t = pl.program_id(1), pl.program_id(2)

    def tile(a_ref, rows, acc_ref, o_ref, extras):
        acc = jnp.dot(a_ref[...], wbf_ref[t], preferred_element_type=F32)
        if nk == 1:
            o_ref[rows, :] = epilogue(acc, rows, *extras).astype(o_ref.dtype)
            return

        @pl.when(t == 0)
        def _():
            acc_ref[rows, :] = acc

        if nk > 2:
            @pl.when(jnp.logical_and(t > 0, t < nk - 1))
            def _():
                acc_ref[rows, :] += acc

        @pl.when(t == nk - 1)
        def _():
            o_ref[rows, :] = epilogue(acc_ref[rows, :] + acc, rows, *extras).astype(o_ref.dtype)

    @pl.when(i == 0)
    def _():
        rows_k = w_refs[0].shape[0]
        for c, w_ref in enumerate(w_refs):
            wbf_ref[t, c * rows_k:(c + 1) * rows_k, :] = w_ref[...].astype(BF16)
        tile(as_ref, slice(None), acc_refs[1] if nk > 1 else None, os_ref, extras_s)

    rows_m = ap_refs[0].shape[0]
    for c, a_ref in enumerate(ap_refs):
        tile(a_ref, slice(c * rows_m, (c + 1) * rows_m), acc_refs[0] if nk > 1 else None, op_ref,
             extras_p)


def _proj(a_p, a_s, w, col0, n_out, dtype_p, dtype_s, epilogue=_epi_plain, extras=(), name="proj"):
    m, k = a_p.shape
    s = a_s.shape[0]
    tm, tn, tk = _tile(m, PROJ_TM), _tile(n_out, PROJ_TN), _tile(k, PROJ_TK)
    nk = k // tk
    n_tiles = n_out // tn
    n_sub = PROJ_SUBTILES if nk == 1 and n_tiles % PROJ_SUBTILES == 0 else 1
    steps = n_sub if nk == 1 else nk
    n_groups = n_tiles // n_sub
    split = nk == 1 and tm % PROJ_DMA_PARTS == 0 and tk % PROJ_DMA_PARTS == 0
    n_parts = PROJ_DMA_PARTS if split else 1
    rows_m, rows_k = tm // n_parts, tk // n_parts
    assert col0 % tn == 0
    off = col0 // tn

    def col(g, i, t):
        return g * n_sub + t if nk == 1 else g

    def col_first_rows_only(g, i, t):
        return jnp.where(i == 0, col(g, i, t), g * n_sub + n_sub - 1) if nk == 1 else g

    def kchunk(t):
        return 0 if nk == 1 else t

    def extra_specs(kind, arr, rows_per_tile, is_prompt):
        if kind == "const":
            return pl.BlockSpec(arr.shape, lambda g, i, t: (0,) * arr.ndim)
        if kind == "rows":
            assert arr.shape[0] % rows_per_tile == 0
            period = arr.shape[0] // rows_per_tile
            if is_prompt:
                return pl.BlockSpec((rows_per_tile, arr.shape[1]), lambda g, i, t: (i % period, 0))
            return pl.BlockSpec((rows_per_tile, arr.shape[1]), lambda g, i, t: (0, 0))
        assert kind == "tile"
        if is_prompt:
            return pl.BlockSpec((rows_per_tile, tn), lambda g, i, t: (i, col(g, i, t)))
        return pl.BlockSpec((rows_per_tile, tn), lambda g, i, t: (0, col_first_rows_only(g, i, t)))

    def a_spec(c):
        return pl.BlockSpec((rows_m, tk), lambda g, i, t: (i * n_parts + c, kchunk(t)))

    def w_index(g, i, t):
        more = g + 1 < n_groups
        if nk == 1:
            ahead = jnp.where(more, (g + 1) * n_sub, g * n_sub + n_sub - 1)
            return 0, jnp.where(i == 0, g * n_sub + t, ahead)
        return (jnp.where(i == 0, t, jnp.where(more, 0, nk - 1)),
                jnp.where(i == 0, g, jnp.minimum(g + 1, n_groups - 1)))

    def w_spec(c):
        def imap(g, i, t):
            kc, tile = w_index(g, i, t)
            return kc * n_parts + c, off + tile
        return pl.BlockSpec((rows_k, tn), imap)

    in_specs = ([a_spec(c) for c in range(n_parts)]
                + [pl.BlockSpec((s, tk), lambda g, i, t: (0, kchunk(t)))]
                + [w_spec(c) for c in range(n_parts)])
    in_specs += [extra_specs(kind, ap, tm, True) for kind, ap, _ in extras]
    in_specs += [extra_specs(kind, asm, s, False) for kind, _, asm in extras]
    operands = ([a_p] * n_parts + [a_s] + [w] * n_parts + [ap for _, ap, _ in extras]
                + [asm for _, _, asm in extras])
    scratch = [pltpu.VMEM((steps, tk, tn), BF16)]
    if nk > 1:
        scratch += [pltpu.VMEM((tm, tn), F32), pltpu.VMEM((s, tn), F32)]
    return pl.pallas_call(
        functools.partial(_proj_kernel, epilogue=epilogue, n_extra=len(extras), nk=nk,
                          n_parts=n_parts),
        grid=(n_groups, m // tm, steps),
        in_specs=in_specs,
        out_specs=[pl.BlockSpec((tm, tn), lambda g, i, t: (i, col(g, i, t))),
                   pl.BlockSpec((s, tn), lambda g, i, t: (0, col_first_rows_only(g, i, t)))],
        out_shape=[jax.ShapeDtypeStruct((m, n_out), dtype_p),
                   jax.ShapeDtypeStruct((s, n_out), dtype_s)],
        scratch_shapes=scratch,
        compiler_params=_params("parallel", "arbitrary", "arbitrary"),
        name=name,
    )(*operands)


def _block_means(page_refs, o_ref, ppb):
    for blk in range(len(page_refs) // ppb):
        tot = jnp.sum(page_refs[blk * ppb][...], axis=0)
        for r in page_refs[blk * ppb + 1:(blk + 1) * ppb]:
            tot = tot + jnp.sum(r[...], axis=0)
        o_ref[blk] = tot * (1.0 / MOBA_BLOCK)


def _moba_prompt_kernel(*refs, scale, n_pages_in, ppb):
    if n_pages_in:
        refs = refs[1:]
    q_ref, k_ref, v_ref, g_ref = refs[:4]
    page_refs = refs[4:4 + n_pages_in]
    o_ref = refs[4 + n_pages_in]
    k16_ref, vt16_ref = refs[-2:]
    if n_pages_in:
        _block_means(page_refs, refs[5 + n_pages_in], ppb)
    seq, hd = q_ref.shape
    blk, topk = MOBA_BLOCK, MOBA_TOPK
    nb = seq // blk
    k16_ref[...] = k_ref[...].astype(BF16)
    vt16_ref[...] = v_ref[...].T.astype(BF16)
    km_rows = SUBLANES * max(1, -(-(nb - 1) // SUBLANES))
    row_id = lax.broadcasted_iota(jnp.int32, (km_rows, hd), 0)
    kmean = jnp.zeros((km_rows, hd), F32)
    for n in range(nb - 1):
        km = jnp.mean(k_ref[n * blk:(n + 1) * blk, :], axis=0, keepdims=True)
        kmean = jnp.where(row_id == n, km, kmean)
    kmean16 = kmean.astype(BF16)
    key_id = lax.broadcasted_iota(jnp.int32, (blk, blk), 0)
    qry_id = lax.broadcasted_iota(jnp.int32, (blk, blk), 1)
    causal_bias = jnp.where(key_id <= qry_id, 0.0, NEG_INF).astype(F32)
    c = scale * math.log2(math.e)

    def scores(j):
        qj = q_ref[j * blk:(j + 1) * blk, :]
        st = lax.dot_general(k16_ref[0:(j + 1) * blk, :], qj, _NT,
                             preferred_element_type=F32)
        gt = None
        if j > topk:
            gt = lax.dot_general(kmean16, qj, _NT, preferred_element_type=F32)
        return st, gt

    def softmax(j, st, gt):
        biases = None
        if j > topk:
            gates = [gt[n:n + 1, :] for n in range(j)]
            ranks = [jnp.zeros((1, blk), F32) for _ in range(j)]
            for lo in range(j):
                for hi in range(lo + 1, j):
                    lo_ahead = jnp.where(gates[lo] >= gates[hi], 1.0, 0.0)
                    ranks[hi] = ranks[hi] + lo_ahead
                    ranks[lo] = ranks[lo] + (1.0 - lo_ahead)
            biases = [jnp.where(r < topk, 0.0, NEG_INF).astype(F32) for r in ranks]
        pieces = []
        for n in range(j + 1):
            sn = st[n * blk:(n + 1) * blk, :]
            if n == j:
                sn = sn + causal_bias
            elif biases is not None:
                sn = sn + biases[n]
            pieces.append(sn)
        mx = jnp.max(pieces[0], axis=0, keepdims=True)
        for sn in pieces[1:]:
            mx = jnp.maximum(mx, jnp.max(sn, axis=0, keepdims=True))
        probs = [jnp.exp2((sn - mx) * c) for sn in pieces]
        denom = jnp.sum(probs[0], axis=0, keepdims=True)
        for p in probs[1:]:
            denom = denom + jnp.sum(p, axis=0, keepdims=True)
        return jnp.concatenate(probs, axis=0).astype(BF16), denom

    def finish(j, pt16, denom):
        rows = slice(j * blk, (j + 1) * blk)
        ot = jnp.dot(vt16_ref[:, 0:(j + 1) * blk], pt16, preferred_element_type=F32) / denom
        o_ref[rows, :] = (ot.T * _silu(g_ref[rows, :].astype(F32))).astype(o_ref.dtype)

    nxt = scores(0)
    pending = None
    for j in range(nb):
        cur = nxt
        if j + 1 < nb:
            nxt = scores(j + 1)
        if pending is not None:
            finish(*pending)
        pending = (j,) + softmax(j, *cur)
    finish(*pending)


def _moba_prompt(q, k, v, g, batch, seq, hd, cache, layer, page_table):
    m, width = q.shape
    heads = width // hd
    assert seq % MOBA_BLOCK == 0 and hd == LANES
    _, _, page, c_heads, c_hd = cache.shape
    ppb = MOBA_BLOCK // page
    total_pages = page_table.size
    steps = batch * heads
    pps = total_pages // steps
    page_bytes = page * c_heads * c_hd * 4
    own_bytes = seq * hd * (2 * (3 * 2 + 2 * 4) + 2 * 2) + 2 * seq * MOBA_BLOCK * 4
    hosted = (total_pages % steps == 0 and pps % ppb == 0 and page_table.shape[1] % ppb == 0
              and own_bytes + 2 * pps * page_bytes <= VMEM_LIMIT_BYTES)
    if not hosted:
        pps = 0
    bps = pps // ppb
    spec = pl.BlockSpec((seq, hd), lambda b, h, *_: (b, h))

    def page_spec(r):
        return pl.BlockSpec((None, None, page, c_heads, c_hd),
                            lambda b, h, pt: (layer, pt[(b * heads + h) * pps + r], 0, 0, 0))

    out_specs, out_shape = [spec], [jax.ShapeDtypeStruct((m, width), BF16)]
    if hosted:
        out_specs.append(pl.BlockSpec((bps, c_heads, c_hd), lambda b, h, pt: (b * heads + h, 0, 0)))
        out_shape.append(jax.ShapeDtypeStruct((total_pages // ppb, c_heads, c_hd), F32))
    outs = pl.pallas_call(
        functools.partial(_moba_prompt_kernel, scale=hd ** -0.5, n_pages_in=pps, ppb=ppb),
        grid_spec=pltpu.PrefetchScalarGridSpec(
            num_scalar_prefetch=1 if hosted else 0,
            grid=(batch, heads),
            in_specs=[spec] * 4 + [page_spec(r) for r in range(pps)],
            out_specs=out_specs,
            scratch_shapes=[pltpu.VMEM((seq, hd), BF16), pltpu.VMEM((hd, seq), BF16)],
        ),
        out_shape=out_shape,
        compiler_params=_params("parallel", "parallel"),
        name="moba_prompt",
    )(*([page_table.reshape(-1)] if hosted else []), q, k, v, g, *([cache] * pps))
    if hosted:
        return outs[0], outs[1].reshape(page_table.shape[0], -1, c_heads, c_hd)
    return outs[0], None


def _kmean_kernel(pt_ref, *refs, ppb):
    *page_refs, o_ref = refs
    _block_means(page_refs, o_ref, ppb)


def _paged_block_means(cache, layer, page_table):
    _, _, page, heads, hd = cache.shape
    db, n_pages = page_table.shape
    ppb = MOBA_BLOCK // page
    assert n_pages % ppb == 0
    nblk = n_pages // ppb
    bps = _tile(nblk, KMEAN_BLOCKS_PER_STEP)
    pps = bps * ppb

    def page_spec(r):
        return pl.BlockSpec((None, None, page, heads, hd),
                            lambda b, n, pt: (layer, pt[b * n_pages + n * pps + r], 0, 0, 0))

    return pl.pallas_call(
        functools.partial(_kmean_kernel, ppb=ppb),
        grid_spec=pltpu.PrefetchScalarGridSpec(
            num_scalar_prefetch=1,
            grid=(db, nblk // bps),
            in_specs=[page_spec(r) for r in range(pps)],
            out_specs=pl.BlockSpec((None, bps, heads, hd), lambda b, n, pt: (b, n, 0, 0)),
        ),
        out_shape=jax.ShapeDtypeStruct((db, nblk, heads, hd), F32),
        compiler_params=_params("parallel", "arbitrary"),
        name="paged_block_means",
    )(page_table.reshape(-1), *([cache] * pps))


def _gate_topk_kernel(q_ref, km_ref, o_ref, *, heads, hd):
    q = q_ref[...].astype(BF16).astype(F32)
    km = km_ref[...].astype(BF16).astype(F32)
    prod = km * q
    nblk = km.shape[0]
    lane = lax.broadcasted_iota(jnp.int32, (nblk, LANES), 1)
    gate = jnp.full((nblk, LANES), NEG_INF, F32)
    for h in range(heads):
        gh = jnp.sum(prod[:, h * hd:(h + 1) * hd], axis=1, keepdims=True)
        gate = jnp.where(lane == h, gh, gate)
    rowf = lax.broadcasted_iota(jnp.int32, (nblk, LANES), 0).astype(F32)
    out_row = lax.broadcasted_iota(jnp.int32, o_ref.shape, 0)
    out = jnp.zeros(o_ref.shape, F32)
    for t in range(MOBA_TOPK):
        mx = jnp.max(gate, axis=0, keepdims=True)
        idx = jnp.min(jnp.where(gate == mx, rowf, float(nblk)), axis=0, keepdims=True)
        out = jnp.where(out_row == t, idx, out)
        gate = jnp.where(rowf == idx, NEG_INF, gate)
    o_ref[...] = out.astype(jnp.int32)


def _gate_topk(q, kmean, heads, hd):
    db, nblk, width = kmean.shape
    assert heads <= LANES and nblk >= MOBA_TOPK
    return pl.pallas_call(
        functools.partial(_gate_topk_kernel, heads=heads, hd=hd),
        grid=(db,),
        in_specs=[pl.BlockSpec((None, 1, width), lambda b: (b, 0, 0)),
                  pl.BlockSpec((None, nblk, width), lambda b: (b, 0, 0))],
        out_specs=pl.BlockSpec((None, SUBLANES, LANES), lambda b: (b, 0, 0)),
        out_shape=jax.ShapeDtypeStruct((db, SUBLANES, LANES), jnp.int32),
        compiler_params=_params("parallel"),
        name="gate_topk",
    )(q, kmean)


def _moba_decode_kernel(sel_ref, pt_ref, q_ref, kn_ref, vn_ref, g_ref, ck_hbm, cv_hbm, o_ref,
                        kbuf, vbuf, sem, *, scale, layer, heads, n_pages, ppb):
    b, hg = pl.program_id(0), pl.program_id(1)
    groups = pl.num_programs(1)
    step = b * groups + hg
    n_steps = pl.num_programs(0) * groups
    slot = step % 2
    hd = kbuf.shape[-1]
    n = MOBA_TOPK * ppb
    group = kbuf.shape[1] // n

    def copies(bb, gg, sl):
        out = []
        for c in range(group):
            hh = gg * group + c
            for t in range(MOBA_TOPK):
                blk = sel_ref[(bb * heads + hh) * MOBA_TOPK + t]
                for r in range(ppb):
                    page = pt_ref[bb * n_pages + blk * ppb + r]
                    i = c * n + t * ppb + r
                    out.append(pltpu.make_async_copy(ck_hbm.at[layer, page, :, hh, :],
                                                     kbuf.at[sl, i], sem.at[sl]))
                    out.append(pltpu.make_async_copy(cv_hbm.at[layer, page, :, hh, :],
                                                     vbuf.at[sl, i], sem.at[sl]))
        return out

    @pl.when(step == 0)
    def _():
        for cp in copies(b, hg, slot):
            cp.start()

    @pl.when(step + 1 < n_steps)
    def _():
        nxt = step + 1
        for cp in copies(nxt // groups, nxt % groups, 1 - slot):
            cp.start()

    for cp in copies(b, hg, slot):
        cp.wait()

    for c in range(group):
        cols = slice(c * hd, (c + 1) * hd)
        q = q_ref[:, cols]
        q16 = jnp.broadcast_to(q, (BF16_SUBLANES, hd)).astype(BF16)
        scores = [lax.dot_general(q16, kbuf[slot, c * n + i].astype(BF16), _NT,
                                  preferred_element_type=F32)[0:1, :] * scale for i in range(n)]
        qb = q.astype(BF16).astype(F32)
        s_new = jnp.sum(qb * kn_ref[:, cols].astype(BF16).astype(F32), axis=1,
                        keepdims=True) * scale
        mx = s_new
        for s in scores:
            mx = jnp.maximum(mx, jnp.max(s, axis=1, keepdims=True))
        p_new = jnp.exp(s_new - mx)
        denom = p_new
        acc = p_new.astype(BF16).astype(F32) * vn_ref[:, cols].astype(BF16).astype(F32)
        for i, s in enumerate(scores):
            p = jnp.exp(s - mx)
            denom = denom + jnp.sum(p, axis=1, keepdims=True)
            p16 = jnp.broadcast_to(p, (BF16_SUBLANES, p.shape[1])).astype(BF16)
            acc = acc + jnp.dot(p16, vbuf[slot, c * n + i].astype(BF16),
                                preferred_element_type=F32)[0:1, :]
        o = acc / denom
        o_ref[:, cols] = (o * _silu(g_ref[:, cols])).astype(o_ref.dtype)


def _moba_decode(q, k_new, v_new, g, cache_k, cache_v, layer, sel, page_table):
    db, _, width = q.shape
    _, _, page, heads, hd = cache_k.shape
    n_pages = page_table.shape[1]
    ppb = MOBA_BLOCK // page
    group = _tile(heads, MOBA_DECODE_HEADS_PER_STEP)
    n = group * MOBA_TOPK * ppb
    row_spec = pl.BlockSpec((None, 1, group * hd), lambda b, h, sel, pt: (b, 0, h))
    hbm_spec = pl.BlockSpec(memory_space=pl.ANY)
    return pl.pallas_call(
        functools.partial(_moba_decode_kernel, scale=hd ** -0.5, layer=layer, heads=heads,
                          n_pages=n_pages, ppb=ppb),
        grid_spec=pltpu.PrefetchScalarGridSpec(
            num_scalar_prefetch=2,
            grid=(db, heads // group),
            in_specs=[row_spec] * 4 + [hbm_spec, hbm_spec],
            out_specs=row_spec,
            scratch_shapes=[pltpu.VMEM((2, n, page, hd), F32), pltpu.VMEM((2, n, page, hd), F32),
                            pltpu.SemaphoreType.DMA((2,))],
        ),
        out_shape=jax.ShapeDtypeStruct((db, 1, width), BF16),
        compiler_params=_params("arbitrary", "arbitrary"),
        name="moba_decode",
    )(sel, page_table.reshape(-1), q, k_new, v_new, g, cache_k, cache_v)


def _head_norm_gate(o, hg, g):
    ms = jnp.mean(o * o, axis=-1, keepdims=True)
    return (o * lax.rsqrt(ms + EPS) * hg) * _silu(g)


def _ret_prompt_kernel(q_ref, k_ref, v_ref, g_ref, dm_ref, qd_ref, kd_ref, cd_ref, hg_ref,
                       o_ref, s_ref):
    c = dm_ref.shape[-1]
    n_chunks = q_ref.shape[0] // c
    s_ref[...] = jnp.zeros(s_ref.shape, F32)
    dm, qdec, kdec, cdec, hg = dm_ref[...], qd_ref[...], kd_ref[...], cd_ref[...], hg_ref[...]

    def local(ci):
        r = slice(ci * c, (ci + 1) * c)
        qc, kc, vc = q_ref[r, :], k_ref[r, :], v_ref[r, :]
        att = lax.dot_general(qc, kc, _NT, preferred_element_type=F32) * dm
        inner = jnp.dot(att.astype(BF16), vc, preferred_element_type=F32)
        qd = (qc.astype(F32) * qdec).astype(BF16)
        kd = (kc.astype(F32) * kdec).astype(BF16)
        return inner, qd, lax.dot_general(kd, vc, _TN, preferred_element_type=F32)

    def carry(ci, inner, qd, kv):
        r = slice(ci * c, (ci + 1) * c)
        s = s_ref[...]
        o = inner + jnp.dot(qd, s.astype(BF16), preferred_element_type=F32)
        s_ref[...] = s * cdec + kv
        o_ref[r, :] = _head_norm_gate(o, hg, g_ref[r, :].astype(F32)).astype(o_ref.dtype)

    nxt = local(0)
    for ci in range(n_chunks):
        cur = nxt
        if ci + 1 < n_chunks:
            nxt = local(ci + 1)
        carry(ci, *cur)


def _ret_tables(heads, c, dk, dv):
    log_g = jnp.log1p(-jnp.exp2(-5.0 - jnp.arange(heads, dtype=F32)))
    i = jnp.arange(c, dtype=F32)
    diff = i[:, None] - i[None, :]
    dmask = jnp.where(diff >= 0, jnp.exp(log_g[:, None, None] * jnp.maximum(diff, 0.0)), 0.0)
    q_dec = jnp.exp(log_g[:, None] * (i[None, :] + 1.0))
    k_dec = jnp.exp(log_g[:, None] * (c - 1.0 - i[None, :]))
    chunk_dec = jnp.exp(log_g * c)
    return (dmask,
            jnp.broadcast_to(q_dec[:, :, None], (heads, c, dk)),
            jnp.broadcast_to(k_dec[:, :, None], (heads, c, dk)),
            jnp.broadcast_to(chunk_dec[:, None, None], (heads, 1, dv)))


def _ret_prompt(q, k, v, g, head_g, batch, seq, heads):
    m = q.shape[0]
    dk, dv = q.shape[1] // heads, v.shape[1] // heads
    c = RET_CHUNK
    assert seq % c == 0
    dmask, qdec, kdec, cdec = _ret_tables(heads, c, dk, dv)
    qk_spec = pl.BlockSpec((seq, dk), lambda b, h: (b, h))
    v_spec = pl.BlockSpec((seq, dv), lambda b, h: (b, h))

    def head_spec(*shape):
        return pl.BlockSpec((None,) + shape, lambda b, h: (h, 0, 0))

    return pl.pallas_call(
        _ret_prompt_kernel,
        grid=(batch, heads),
        in_specs=[qk_spec, qk_spec, v_spec, v_spec, head_spec(c, c), head_spec(c, dk),
                  head_spec(c, dk), head_spec(1, dv), pl.BlockSpec((1, dv), lambda b, h: (0, 0))],
        out_specs=[v_spec, pl.BlockSpec((None, None, dk, dv), lambda b, h: (b, h, 0, 0))],
        out_shape=[jax.ShapeDtypeStruct((m, heads * dv), BF16),
                   jax.ShapeDtypeStruct((batch, heads, dk, dv), F32)],
        compiler_params=_params("parallel", "parallel"),
        name="ret_prompt",
    )(q, k, v, g, dmask, qdec, kdec, cdec, head_g.reshape(1, dv))


def _ret_decode_kernel(q_ref, k_ref, kcol_ref, v_ref, g_ref, s0_ref, gam_ref, hg_ref,
                       o_ref, s_ref):
    group, dk, dv = s0_ref.shape
    for c in range(group):
        qk = slice(c * dk, (c + 1) * dk)
        vv = slice(c * dv, (c + 1) * dv)
        q, k, v, s0, gam = q_ref[:, qk], k_ref[:, qk], v_ref[:, vv], s0_ref[c], gam_ref[c]
        att = jnp.sum(q.astype(BF16).astype(F32) * k.astype(BF16).astype(F32), axis=1,
                      keepdims=True)
        qd = jnp.broadcast_to(q * gam[:, :dk], (BF16_SUBLANES, dk)).astype(BF16)
        o = att * v + jnp.dot(qd, s0.astype(BF16), preferred_element_type=F32)[0:1, :]
        s_ref[c] = s0 * gam + kcol_ref[c] * v
        o_ref[:, vv] = _head_norm_gate(o, hg_ref[...], g_ref[:, vv]).astype(o_ref.dtype)


def _ret_decode(q, k, v, g, s0, head_g, heads):
    db = q.shape[0]
    dk, dv = q.shape[2] // heads, v.shape[2] // heads
    assert dv >= dk
    group = _tile(heads, RET_DECODE_HEADS_PER_STEP)
    gamma = jnp.exp(jnp.log1p(-jnp.exp2(-5.0 - jnp.arange(heads, dtype=F32))) * 1.0)
    gam = jnp.broadcast_to(gamma[:, None, None], (heads, 1, dv))
    kcol = k.reshape(db, heads, dk, 1)
    qk_spec = pl.BlockSpec((None, 1, group * dk), lambda b, h: (b, 0, h))
    v_spec = pl.BlockSpec((None, 1, group * dv), lambda b, h: (b, 0, h))
    s_spec = pl.BlockSpec((None, group, dk, dv), lambda b, h: (b, h, 0, 0))
    return pl.pallas_call(
        _ret_decode_kernel,
        grid=(db, heads // group),
        in_specs=[qk_spec, qk_spec,
                  pl.BlockSpec((None, group, dk, 1), lambda b, h: (b, h, 0, 0)),
                  v_spec, v_spec, s_spec,
                  pl.BlockSpec((group, 1, dv), lambda b, h: (h, 0, 0)),
                  pl.BlockSpec((1, dv), lambda b, h: (0, 0))],
        out_specs=[v_spec, s_spec],
        out_shape=[jax.ShapeDtypeStruct((db, 1, heads * dv), BF16),
                   jax.ShapeDtypeStruct(s0.shape, F32)],
        compiler_params=_params("parallel", "parallel"),
        name="ret_decode",
    )(q, k, kcol, v, g, s0, gam, head_g.reshape(1, dv))


def _rope_tables(pos, dim):
    half = dim // 2
    inv = ROPE_THETA ** (-jnp.arange(half, dtype=F32) / half)
    ang = pos.astype(F32)[:, None] * inv[None, :]
    return jnp.cos(ang), jnp.sin(ang)


def _moba_in_proj(h_p, h_s, w, qn, kn, pos_p, pos_s, width, hd):
    def tables(pos):
        cos, sin = _rope_tables(pos, hd)
        return jnp.concatenate([cos, cos], axis=-1), jnp.concatenate([-sin, sin], axis=-1)

    (cos_p, sin_p), (cos_s, sin_s) = tables(pos_p), tables(pos_s)
    rope = [("rows", cos_p, cos_s), ("rows", sin_p, sin_s)]
    qn, kn = qn.reshape(1, hd), kn.reshape(1, hd)
    q = _proj(h_p, h_s, w, 0, width, BF16, F32, _epi_headnorm_rope,
              [("const", qn, qn)] + rope, name="attn_q_proj")
    k = _proj(h_p, h_s, w, width, width, F32, F32, _epi_headnorm_rope,
              [("const", kn, kn)] + rope, name="attn_k_proj")
    v = _proj(h_p, h_s, w, 2 * width, width, F32, F32, name="attn_v_proj")
    g = _proj(h_p, h_s, w, 3 * width, width, BF16, F32, name="attn_g_proj")
    return q, k, v, g


def _ret_in_proj(h_p, h_s, w, pos_p, pos_s, qk_width, v_width, dk):
    (cos_p, sin_p), (cos_s, sin_s) = _rope_tables(pos_p, dk), _rope_tables(pos_s, dk)
    rope = [("rows", cos_p, cos_s), ("rows", sin_p, sin_s)]
    q = _proj(h_p, h_s, w, 0, qk_width, BF16, F32,
              functools.partial(_epi_rope_wide, scale=1.0), rope, name="ret_q_proj")
    k = _proj(h_p, h_s, w, qk_width, qk_width, BF16, F32,
              functools.partial(_epi_rope_wide, scale=dk ** -0.5), rope, name="ret_k_proj")
    v = _proj(h_p, h_s, w, 2 * qk_width, v_width, BF16, F32, name="ret_v_proj")
    g = _proj(h_p, h_s, w, 2 * qk_width + v_width, v_width, BF16, F32, name="ret_g_proj")
    return q, k, v, g


def _residual_proj(a_p, a_s, w, x_p, x_s, name):
    return _proj(a_p, a_s, w, 0, w.shape[1], F32, F32, _epi_residual, [("tile", x_p, x_s)],
                 name=name)


def kernel(x_prompt, x_sample, cache_k, cache_v, state_ret, page_table, attn_norm, w_in_attn,
           q_norm, k_norm, w_out_attn, ret_norm, w_in_ret, ret_head_norm, w_out_ret):
    batch, seq, d = x_prompt.shape
    db, dec_seq, _ = x_sample.shape
    assert dec_seq == 1
    n_attn, n_ret = attn_norm.shape[0], ret_norm.shape[0]
    hd = q_norm.shape[-1]
    width = w_out_attn.shape[1]
    heads = width // hd
    dv = ret_head_norm.shape[-1]
    v_width = w_out_ret.shape[1]
    ret_heads = v_width // dv
    qk_width = (w_in_ret.shape[-1] - 2 * v_width) // 2
    dk = qk_width // ret_heads
    assert dk == 2 * LANES
    n_pages = page_table.shape[1]
    past_len = n_pages * cache_k.shape[2]
    assert cache_k.shape[2] == PAGE_SIZE and past_len % MOBA_BLOCK == 0
    assert cache_k.shape[3:] == (heads, hd)

    m = batch * seq
    sp = BF16_SUBLANES * (-(-db // BF16_SUBLANES))
    xp = x_prompt.reshape(m, d)
    xs = jnp.pad(x_sample.reshape(db, d), ((0, sp - db), (0, 0)))
    pos_p = jnp.arange(seq, dtype=jnp.int32)
    pos_s = jnp.full((sp,), past_len, jnp.int32)

    def pad_rows(t):
        return jnp.pad(t.reshape(db, -1), ((0, sp - db), (0, 0)))

    kp_l, vp_l, ks_l, vs_l, sp_l, ss_l = [], [], [], [], [], []
    for layer in range(n_attn + n_ret):
        j = layer // 2
        if layer % 2 == 0:
            h_p, h_s = _rmsnorm_bf16(xp, attn_norm[j]), _rmsnorm_bf16(xs, attn_norm[j])
            (q_p, q_s), (k_p, k_s), (v_p, v_s), (g_p, g_s) = _moba_in_proj(
                h_p, h_s, w_in_attn[j], q_norm[j], k_norm[j], pos_p, pos_s, width, hd)
            o_p, kmean = _moba_prompt(q_p, k_p, v_p, g_p, batch, seq, hd, cache_k, j, page_table)
            q_s, k_s, v_s, g_s = (t[:db].reshape(db, 1, width) for t in (q_s, k_s, v_s, g_s))
            if kmean is None:
                kmean = _paged_block_means(cache_k, j, page_table)
            kmean = kmean.reshape(db, kmean.shape[1], width)
            sel = _gate_topk(q_s, kmean, heads, hd)[:, :MOBA_TOPK, :heads]
            sel = jnp.swapaxes(sel, 1, 2).reshape(-1)
            o_s = _moba_decode(q_s, k_s, v_s, g_s, cache_k, cache_v, j, sel, page_table)
            xp, xs = _residual_proj(o_p, pad_rows(o_s), w_out_attn[j], xp, xs, "attn_out_proj")
            kp_l.append(k_p.reshape(batch, seq, heads, hd))
            vp_l.append(v_p.reshape(batch, seq, heads, hd))
            ks_l.append(k_s.reshape(db, 1, heads, hd))
            vs_l.append(v_s.reshape(db, 1, heads, hd))
        else:
            h_p, h_s = _rmsnorm_bf16(xp, ret_norm[j]), _rmsnorm_bf16(xs, ret_norm[j])
            (q_p, q_s), (k_p, k_s), (v_p, v_s), (g_p, g_s) = _ret_in_proj(
                h_p, h_s, w_in_ret[j], pos_p, pos_s, qk_width, v_width, dk)
            o_p, s_p = _ret_prompt(q_p, k_p, v_p, g_p, ret_head_norm[j], batch, seq, ret_heads)
            q_s, k_s, v_s, g_s = (t[:db].reshape(db, 1, -1) for t in (q_s, k_s, v_s, g_s))
            o_s, s_s = _ret_decode(q_s, k_s, v_s, g_s, state_ret[j], ret_head_norm[j], ret_heads)
            xp, xs = _residual_proj(o_p, pad_rows(o_s), w_out_ret[j], xp, xs, "ret_out_proj")
            sp_l.append(s_p)
            ss_l.append(s_s)

    return (xp.reshape(batch, seq, d), xs[:db].reshape(db, 1, d),
            jnp.stack(kp_l), jnp.stack(vp_l), jnp.stack(ks_l), jnp.stack(vs_l),
            jnp.stack(sp_l), jnp.stack(ss_l))
```
